```python
import math
import jax
import jax.numpy as jnp
from jax import lax
import numpy as np

D_MODEL = 2048
BATCH = 4
SEQ = 4096
DEPTH = 2
DEC_BATCH = 128
DEC_SEQ = 8
PAST_LEN = 16384
PAGE_SIZE = 128

MIX = D_MODEL
A_HEADS = 16
A_NOPE = 64
A_ROPE = 32
A_VDIM = 64
A_QRANK = 384
A_KVRANK = 128
A_LAT = A_KVRANK + A_ROPE
ROPE_BASE = 10000.0
B_HEADS = 4
B_D = 64
C_HEADS = 8
C_DK = 64
C_DV = 64
CMP_BLOCK = 64
N_SEL = 16
WINDOW = 512
N_MEM = 256
M_HEADS = 4
M_DH = 128
D_FF = 5632
Q_BLOCK = 128
EPS = 1e-6
IN_SIZES = (A_QRANK, A_KVRANK, A_ROPE, B_HEADS * 2 * B_D, 4 * B_D, C_HEADS * C_DK,
            C_DK + C_DV, C_DK + C_DV, C_DK + C_DV, 3 * C_HEADS)
N_IN = sum(IN_SIZES)
IN_SPLIT = [int(v) for v in np.cumsum(IN_SIZES)[:-1]]

kernel_name = 'hybrid_mla_diff_nsa_step'


def rmsnorm(x, g):
    xf = x.astype(jnp.float32)
    y = xf * lax.rsqrt(jnp.mean(xf * xf, axis=-1, keepdims=True) + EPS)
    return (y * g.astype(jnp.float32)).astype(x.dtype)


def swiglu(x, wi, wo):
    g, u = jnp.split(x @ wi, 2, axis=-1)
    return (jax.nn.silu(g) * u) @ wo


def rope(x, pos):
    half = x.shape[-1] // 2
    inv = ROPE_BASE ** (-jnp.arange(half, dtype=jnp.float32) / half)
    ang = pos.astype(jnp.float32)[:, None] * inv
    ang = ang.reshape(ang.shape[:1] + (1,) * (x.ndim - 3) + (half,))
    cos, sin = jnp.cos(ang), jnp.sin(ang)
    x1 = x[..., :half].astype(jnp.float32)
    x2 = x[..., half:].astype(jnp.float32)
    return jnp.concatenate([x1 * cos - x2 * sin, x1 * sin + x2 * cos], axis=-1).astype(x.dtype)


def alibi_slopes(n):
    return jnp.asarray([2.0 ** (-8.0 * (i + 1) / n) for i in range(n)], dtype=jnp.float32)


def masked_softmax(s, mask):
    s = jnp.where(mask, s, -jnp.inf)
    m = jnp.max(s, axis=-1, keepdims=True)
    m = jnp.where(jnp.isfinite(m), m, 0.0)
    e = jnp.where(mask, jnp.exp(s - m), 0.0)
    return e / jnp.maximum(jnp.sum(e, axis=-1, keepdims=True), 1e-30)


def project(h, pos, lw):
    b, t = h.shape[:2]
    qc, ckv, kr, qb, dkv, qn, cmp_r, sel_r, win_r, gl = jnp.split(h @ lw['w_in'], IN_SPLIT, axis=-1)
    q = (rmsnorm(qc, lw['mla_q_norm']) @ lw['mla_w_uq']).reshape(b, t, A_HEADS, A_NOPE + A_ROPE)
    q_lat = jnp.einsum('bthn,chn->bthc', q[..., :A_NOPE], lw['mla_w_uk'])
    q_rope = rope(q[..., A_NOPE:], pos)
    lat = jnp.concatenate([rmsnorm(ckv, lw['mla_kv_norm']), rope(kr, pos)], axis=-1)
    qb = qb.reshape(b, t, B_HEADS, 2, B_D)
    qn = qn.reshape(b, t, C_HEADS, C_DK)
    gates = jax.nn.sigmoid(gl.reshape(b, t, C_HEADS, 3))
    return q_lat, q_rope, lat, qb[..., 0, :], qb[..., 1, :], dkv, qn, cmp_r, sel_r, win_r, gates


def mla_core(q_lat, q_rope, lat, qpos, kpos):
    c, kr = lat[..., :A_KVRANK], lat[..., A_KVRANK:]
    s = jnp.einsum('bqhc,bkc->bhqk', q_lat, c) + jnp.einsum('bqhr,bkr->bhqk', q_rope, kr)
    p = masked_softmax(s.astype(jnp.float32) * (A_NOPE + A_ROPE) ** -0.5, kpos[None, :] <= qpos[:, None])
    return jnp.einsum('bhqk,bkc->bqhc', p.astype(c.dtype), c)


def diff_core(q1, q2, dkv, qpos, kpos, lam, slopes):
    k1, k2, v = dkv[..., :B_D], dkv[..., B_D:2 * B_D], dkv[..., 2 * B_D:]
    dist = (qpos[:, None] - kpos[None, :]).astype(jnp.float32)
    mask = dist >= 0
    bias = -slopes[:, None, None] * dist
    scale = B_D ** -0.5
    s1 = jnp.einsum('bqhd,bkd->bhqk', q1, k1).astype(jnp.float32) * scale + bias
    s2 = jnp.einsum('bqhd,bkd->bhqk', q2, k2).astype(jnp.float32) * scale + bias
    p = masked_softmax(s1, mask) - lam * masked_softmax(s2, mask)
    return jnp.einsum('bhqk,bkv->bqhv', p.astype(v.dtype), v)


def pool_blocks(rows, w):
    b, L, c = rows.shape
    blk = jnp.einsum('bnjc,j->bnc', rows.reshape(b, L // CMP_BLOCK, CMP_BLOCK, c), w)
    return blk[..., :C_DK], blk[..., C_DK:]


def nsa_core(q, qpos, kc, vc, fetch_sel, w_rows, wpos, gates, slopes):
    b, tq = q.shape[:2]
    scale = C_DK ** -0.5
    nb = kc.shape[1]
    cur = qpos // CMP_BLOCK
    blk = jnp.arange(nb)
    cmask = blk[None, :] < cur[:, None]
    cdist = (qpos[:, None] - ((blk + 1) * CMP_BLOCK - 1)[None, :]).astype(jnp.float32)
    sc = jnp.einsum('bqhd,bnd->bhqn', q, kc).astype(jnp.float32) * scale - slopes[:, None, None] * cdist
    pc = masked_softmax(sc, cmask)
    o_cmp = jnp.einsum('bhqn,bnd->bqhd', pc.astype(vc.dtype), vc)
    imp = jnp.where(cmask, jnp.sum(pc, axis=1), -jnp.inf)
    _, sel = lax.top_k(imp, min(N_SEL, nb))
    offs = jnp.arange(CMP_BLOCK)
    pos_sel = sel[..., None] * CMP_BLOCK + offs
    val_sel = jnp.broadcast_to((sel < cur[None, :, None])[..., None], pos_sel.shape)
    pos_cur = jnp.broadcast_to((cur * CMP_BLOCK)[None, :, None, None] + offs, (b, tq, 1, CMP_BLOCK))
    val_cur = pos_cur <= qpos[None, :, None, None]
    spos = jnp.concatenate([pos_sel, pos_cur], axis=2).reshape(b, tq, -1)
    sval = jnp.concatenate([val_sel, val_cur], axis=2).reshape(b, tq, -1)
    ks, vs = fetch_sel(spos)
    sdist = (qpos[None, :, None] - spos).astype(jnp.float32)
    ss = jnp.einsum('bqhd,bqmd->bhqm', q, ks).astype(jnp.float32) * scale - slopes[None, :, None, None] * sdist[:, None]
    ps = masked_softmax(ss, sval[:, None])
    o_sel = jnp.einsum('bhqm,bqmd->bqhd', ps.astype(vs.dtype), vs)
    kw, vw = w_rows[..., :C_DK], w_rows[..., C_DK:]
    wdist = qpos[:, None] - wpos[None, :]
    wmask = (wdist >= 0) & (wdist <= WINDOW) & (wpos[None, :] >= 0)
    sw = jnp.einsum('bqhd,bkd->bhqk', q, kw).astype(jnp.float32) * scale - slopes[:, None, None] * wdist.astype(jnp.float32)
    pw = masked_softmax(sw, wmask)
    o_win = jnp.einsum('bhqk,bkd->bqhd', pw.astype(vw.dtype), vw)
    return gates[..., 0:1] * o_cmp + gates[..., 1:2] * o_sel + gates[..., 2:3] * o_win


def merge_heads(o_lat, o_dif, o_nsa, lw, lam_init):
    b, t = o_lat.shape[:2]
    o_a = jnp.einsum('bqhc,chv->bqhv', o_lat, lw['mla_w_uv']).reshape(b, t, A_HEADS * A_VDIM)
    o_b = (rmsnorm(o_dif, lw['diff_norm']) * (1.0 - lam_init)).reshape(b, t, B_HEADS * 2 * B_D)
    o_c = o_nsa.reshape(b, t, C_HEADS * C_DV)
    return jnp.concatenate([o_a, o_b, o_c], axis=-1)


def prompt_mixer(h, pos, lw, lam, lam_init, slopes_b, slopes_c):
    b, t = h.shape[:2]
    q_lat, q_rope, lat, q1, q2, dkv, qn, cmp_r, sel_r, win_r, gates = project(h, pos, lw)
    kc, vc = pool_blocks(cmp_r, lw['nsa_w_cmp'])
    win_pad = jnp.pad(win_r, ((0, 0), (WINDOW, 0), (0, 0)))
    nqb = t // Q_BLOCK

    def fetch(spos):
        rows = jax.vmap(lambda r, i: r[i])(sel_r, jnp.clip(spos, 0, t - 1))
        return rows[..., :C_DK], rows[..., C_DK:]

    def to_blocks(a):
        return jnp.moveaxis(a.reshape((b, nqb, Q_BLOCK) + a.shape[2:]), 1, 0)

    def body(args):
        i, ql, qr, q1b, q2b, qnb, gb = args
        qpos = i * Q_BLOCK + jnp.arange(Q_BLOCK)
        o_lat = mla_core(ql, qr, lat, qpos, pos)
        o_dif = diff_core(q1b, q2b, dkv, qpos, pos, lam, slopes_b)
        w_rows = lax.dynamic_slice_in_dim(win_pad, i * Q_BLOCK, WINDOW + Q_BLOCK, axis=1)
        wpos = i * Q_BLOCK - WINDOW + jnp.arange(WINDOW + Q_BLOCK)
        o_nsa = nsa_core(qnb, qpos, kc, vc, fetch, w_rows, wpos, gb, slopes_c)
        return merge_heads(o_lat, o_dif, o_nsa, lw, lam_init)

    out = lax.map(body, (jnp.arange(nqb), to_blocks(q_lat), to_blocks(q_rope), to_blocks(q1),
                         to_blocks(q2), to_blocks(qn), to_blocks(gates)))
    out = jnp.moveaxis(out, 0, 1).reshape(b, t, MIX)
    return out @ lw['w_out'], (lat, dkv, cmp_r, sel_r, win_r[:, -min(WINDOW, t):])


def sample_mixer(h, pos, lw, lam, lam_init, slopes_b, slopes_c, layer,
                 cache_mla, cache_diff, cache_nsa_cmp, cache_nsa_sel, win_state, page_table):
    b, t = h.shape[:2]
    q_lat, q_rope, lat, q1, q2, dkv, qn, cmp_r, sel_r, win_r, gates = project(h, pos, lw)
    n_cmp = (PAST_LEN + t) // CMP_BLOCK
    wb = win_state.shape[1]
    kpos = jnp.arange(PAST_LEN + t)
    wpos = jnp.concatenate([PAST_LEN - wb + jnp.arange(wb), pos])

    def past(cache, pt):
        return cache[layer, pt].reshape(PAST_LEN, cache.shape[-1])

    def body(args):
        pt, ql, qr, lat_n, q1n, q2n, dkv_n, qnn, cmp_n, sel_n, win_n, g_n, buf = args
        lat_all = jnp.concatenate([past(cache_mla, pt), lat_n], axis=0)[None]
        o_lat = mla_core(ql[None], qr[None], lat_all, pos, kpos)
        dkv_all = jnp.concatenate([past(cache_diff, pt), dkv_n], axis=0)[None]
        o_dif = diff_core(q1n[None], q2n[None], dkv_all, pos, kpos, lam, slopes_b)
        cmp_all = jnp.concatenate([past(cache_nsa_cmp, pt), cmp_n], axis=0)[None, :n_cmp * CMP_BLOCK]
        kc, vc = pool_blocks(cmp_all, lw['nsa_w_cmp'])

        def fetch(spos):
            pp = jnp.clip(spos, 0, PAST_LEN - 1)
            rows_past = cache_nsa_sel[layer, pt[pp // PAGE_SIZE], pp % PAGE_SIZE]
            rows_new = sel_n[jnp.clip(spos - PAST_LEN, 0, t - 1)]
            rows = jnp.where((spos < PAST_LEN)[..., None], rows_past, rows_new)
            return rows[..., :C_DK], rows[..., C_DK:]

        w_rows = jnp.concatenate([buf, win_n], axis=0)[None]
        o_nsa = nsa_core(qnn[None], pos, kc, vc, fetch, w_rows, wpos, g_n[None], slopes_c)
        return merge_heads(o_lat, o_dif, o_nsa, lw, lam_init)[0]

    out = lax.map(body, (page_table, q_lat, q_rope, lat, q1, q2, dkv, qn, cmp_r, sel_r, win_r, gates, win_state))
    new_win = jnp.concatenate([win_state, win_r], axis=1)[:, -wb:]
    return out @ lw['w_out'], (lat, dkv, cmp_r, sel_r, new_win)


def cross_attn(h, mem_kv, wq, wo):
    b, t = h.shape[:2]
    q = (h @ wq).reshape(b, t, M_HEADS, M_DH)
    k = mem_kv[..., :M_HEADS * M_DH].reshape(b, -1, M_HEADS, M_DH)
    v = mem_kv[..., M_HEADS * M_DH:].reshape(b, -1, M_HEADS, M_DH)
    s = jnp.einsum('bqhd,bkhd->bhqk', q, k).astype(jnp.float32) * M_DH ** -0.5
    p = jax.nn.softmax(s, axis=-1)
    o = jnp.einsum('bhqk,bkhd->bqhd', p.astype(v.dtype), v).reshape(b, t, M_HEADS * M_DH)
    return o @ wo


def setup_inputs(seed: int = 0) -> dict:
    key = jax.random.key(seed)
    keys = jax.random.split(key, 40)
    counter = iter(range(40))

    def nk():
        return keys[next(counter)]

    def randn(shape):
        return jax.random.normal(nk(), shape, jnp.float32)

    def normal(shape, scale):
        return jax.random.normal(nk(), shape, jnp.float32) * scale

    def gain(shape):
        return 1.0 + 0.02 * jax.random.normal(nk(), shape, jnp.float32)

    n_pages = PAST_LEN // PAGE_SIZE
    n_used = DEC_BATCH * n_pages
    n_pool = n_used + max(1, n_used // 4)
    wb = min(WINDOW, PAST_LEN)
    row = C_DK + C_DV
    page_table = jax.random.permutation(nk(), n_pool)[:n_used].reshape(DEC_BATCH, n_pages).astype(jnp.int32)
    return {
        'x_prompt': randn((BATCH, SEQ, D_MODEL)),
        'x_sample': randn((DEC_BATCH, DEC_SEQ, D_MODEL)),
        'mem_prompt': randn((BATCH, N_MEM, D_MODEL)),
        'cache_mla': randn((DEPTH, n_pool, PAGE_SIZE, A_LAT)),
        'cache_diff': randn((DEPTH, n_pool, PAGE_SIZE, 4 * B_D)),
        'cache_nsa_cmp': randn((DEPTH, n_pool, PAGE_SIZE, row)),
        'cache_nsa_sel': randn((DEPTH, n_pool, PAGE_SIZE, row)),
        'state_nsa_win': randn((DEPTH, DEC_BATCH, wb, row)),
        'cache_mem': randn((DEPTH, DEC_BATCH, N_MEM, 2 * M_HEADS * M_DH)),
        'page_table': page_table,
        'ffn1_norm': gain((DEPTH, D_MODEL)),
        'ffn1_wi': normal((DEPTH, D_MODEL, 2 * D_FF), D_MODEL ** -0.5),
        'ffn1_wo': normal((DEPTH, D_FF, D_MODEL), D_FF ** -0.5),
        'mix_norm': gain((DEPTH, D_MODEL)),
        'w_in': normal((DEPTH, D_MODEL, N_IN), D_MODEL ** -0.5),
        'mla_q_norm': gain((DEPTH, A_QRANK)),
        'mla_w_uq': normal((DEPTH, A_QRANK, A_HEADS * (A_NOPE + A_ROPE)), A_QRANK ** -0.5),
        'mla_kv_norm': gain((DEPTH, A_KVRANK)),
        'mla_w_uk': normal((DEPTH, A_KVRANK, A_HEADS, A_NOPE), A_KVRANK ** -0.5),
        'mla_w_uv': normal((DEPTH, A_KVRANK, A_HEADS, A_VDIM), A_KVRANK ** -0.5),
        'diff_lam': normal((DEPTH, 4, B_D), 0.1),
        'diff_norm': gain((DEPTH, 2 * B_D)),
        'nsa_w_cmp': (1.0 + 0.1 * randn((DEPTH, CMP_BLOCK))) / CMP_BLOCK,
        'w_out': normal((DEPTH, MIX, D_MODEL), MIX ** -0.5),
        'xattn_norm': gain((DEPTH, D_MODEL)),
        'mem_norm': gain((DEPTH, D_MODEL)),
        'xattn_wq': normal((DEPTH, D_MODEL, M_HEADS * M_DH), D_MODEL ** -0.5),
        'xattn_wkv': normal((DEPTH, D_MODEL, 2 * M_HEADS * M_DH), D_MODEL ** -0.5),
        'xattn_wo': normal((DEPTH, M_HEADS * M_DH, D_MODEL), (M_HEADS * M_DH) ** -0.5),
        'ffn2_norm': gain((DEPTH, D_MODEL)),
        'ffn2_wi': normal((DEPTH, D_MODEL, 2 * D_FF), D_MODEL ** -0.5),
        'ffn2_wo': normal((DEPTH, D_FF, D_MODEL), D_FF ** -0.5),
        'final_norm': gain((D_MODEL,)),
    }


def reference(x_prompt, x_sample, mem_prompt, cache_mla, cache_diff, cache_nsa_cmp, cache_nsa_sel,
              state_nsa_win, cache_mem, page_table, ffn1_norm, ffn1_wi, ffn1_wo, mix_norm, w_in,
              mla_q_norm, mla_w_uq, mla_kv_norm, mla_w_uk, mla_w_uv, diff_lam, diff_norm, nsa_w_cmp,
              w_out, xattn_norm, mem_norm, xattn_wq, xattn_wkv, xattn_wo, ffn2_norm, ffn2_wi, ffn2_wo,
              final_norm):
    pos_p = jnp.arange(x_prompt.shape[1])
    pos_s = PAST_LEN + jnp.arange(x_sample.shape[1])
    slopes_b = alibi_slopes(B_HEADS)
    slopes_c = alibi_slopes(C_HEADS)
    xp, xs = x_prompt, x_sample
    mla_p, mla_s, diff_p, diff_s = [], [], [], []
    cmp_p, cmp_s, sel_p, sel_s = [], [], [], []
    win_p, win_s, mem_p = [], [], []
    for l in range(DEPTH):
        lw = {'w_in': w_in[l], 'mla_q_norm': mla_q_norm[l], 'mla_w_uq': mla_w_uq[l],
              'mla_kv_norm': mla_kv_norm[l], 'mla_w_uk': mla_w_uk[l], 'mla_w_uv': mla_w_uv[l],
              'diff_norm': diff_norm[l], 'nsa_w_cmp': nsa_w_cmp[l], 'w_out': w_out[l]}
        lam_init = 0.8 - 0.6 * math.exp(-0.3 * l)
        dl = diff_lam[l].astype(jnp.float32)
        lam = jnp.exp(jnp.sum(dl[0] * dl[1])) - jnp.exp(jnp.sum(dl[2] * dl[3])) + lam_init
        xp = xp + 0.5 * swiglu(rmsnorm(xp, ffn1_norm[l]), ffn1_wi[l], ffn1_wo[l])
        xs = xs + 0.5 * swiglu(rmsnorm(xs, ffn1_norm[l]), ffn1_wi[l], ffn1_wo[l])
        mix, rows_p = prompt_mixer(rmsnorm(xp, mix_norm[l]), pos_p, lw, lam, lam_init, slopes_b, slopes_c)
        xp = xp + mix
        mix, rows_s = sample_mixer(rmsnorm(xs, mix_norm[l]), pos_s, lw, lam, lam_init, slopes_b, slopes_c, l,
                                   cache_mla, cache_diff, cache_nsa_cmp, cache_nsa_sel, state_nsa_win[l], page_table)
        xs = xs + mix
        mem_kv = rmsnorm(mem_prompt, mem_norm[l]) @ xattn_wkv[l]
        xp = xp + cross_attn(rmsnorm(xp, xattn_norm[l]), mem_kv, xattn_wq[l], xattn_wo[l])
        xs = xs + cross_attn(rmsnorm(xs, xattn_norm[l]), cache_mem[l], xattn_wq[l], xattn_wo[l])
        xp = xp + 0.5 * swiglu(rmsnorm(xp, ffn2_norm[l]), ffn2_wi[l], ffn2_wo[l])
        xs = xs + 0.5 * swiglu(rmsnorm(xs, ffn2_norm[l]), ffn2_wi[l], ffn2_wo[l])
        mla_p.append(rows_p[0]); diff_p.append(rows_p[1]); cmp_p.append(rows_p[2])
        sel_p.append(rows_p[3]); win_p.append(rows_p[4]); mem_p.append(mem_kv)
        mla_s.append(rows_s[0]); diff_s.append(rows_s[1]); cmp_s.append(rows_s[2])
        sel_s.append(rows_s[3]); win_s.append(rows_s[4])
    y_prompt = rmsnorm(xp, final_norm)
    y_sample = rmsnorm(xs, final_norm)
    return (y_prompt, y_sample,
            jnp.stack(mla_p), jnp.stack(mla_s), jnp.stack(diff_p), jnp.stack(diff_s),
            jnp.stack(cmp_p), jnp.stack(cmp_s), jnp.stack(sel_p), jnp.stack(sel_s),
            jnp.stack(win_p), jnp.stack(win_s), jnp.stack(mem_p))
```

```python
import functools
import math

import numpy as np
import jax
import jax.numpy as jnp
from jax import lax
from jax.experimental import pallas as pl
from jax.experimental.pallas import tpu as pltpu

F32 = jnp.float32
BF16 = jnp.bfloat16

A_HEADS, A_NOPE, A_ROPE, A_VDIM, A_QRANK, A_KVRANK = 16, 64, 32, 64, 384, 128
A_LAT = A_KVRANK + A_ROPE
ROPE_BASE = 10000.0
B_HEADS, B_D = 4, 64
C_HEADS, C_DK, C_DV = 8, 64, 64
CMP_BLOCK, N_SEL, WINDOW = 64, 16, 512
M_HEADS, M_DH = 4, 128
PAGE_SIZE = 128
EPS = 1e-6
IN_SIZES = (A_QRANK, A_KVRANK, A_ROPE, B_HEADS * 2 * B_D, 4 * B_D, C_HEADS * C_DK,
            C_DK + C_DV, C_DK + C_DV, C_DK + C_DV, 3 * C_HEADS)

LANE = 128
SUBLANE = 8
VMEM_LIMIT = 56 * 1024 * 1024

MASKED = -2e30
M_INIT = -1e30

S_QC = 0
S_CKV = S_QC + A_QRANK
S_QB = S_CKV + LANE
S_DKV = S_QB + 2 * B_HEADS * LANE
S_QN = S_DKV + 4 * B_D
S_CMP = S_QN + C_HEADS * LANE
S_SEL = S_CMP + LANE
S_WIN = S_SEL + LANE
S_KR = S_WIN + LANE
S_KRR = S_KR + LANE
S_GL = S_KRR + LANE
N_WALL = S_GL + LANE

MIX_A = A_HEADS * A_VDIM
MIX_B = B_HEADS * LANE
MIX_C = C_HEADS * LANE
MIX_W = MIX_A + MIX_B + MIX_C


def _cparams(sem):
    return pltpu.CompilerParams(dimension_semantics=sem, vmem_limit_bytes=VMEM_LIMIT)


def _rms(x, g):
    return x * lax.rsqrt(jnp.mean(x * x, axis=-1, keepdims=True) + EPS) * g


def _dot(a, b):
    return jnp.dot(a, b, preferred_element_type=F32)


def _dot_t(a, b):
    return lax.dot_general(a, b, (((1,), (1,)), ((), ())), preferred_element_type=F32)


def _rep(x, n):
    if n == LANE:
        return x
    if n < LANE:
        return x[:, :n]
    return jnp.tile(x, (1, n // LANE))


def _log2(n):
    assert n > 0 and n & (n - 1) == 0, n
    return n.bit_length() - 1


def _div(x, n):
    return lax.shift_right_logical(x, _log2(n))


def _mod(x, n):
    return x & (n - 1)


def _alibi_col(rows, rows_per_head, n_heads):
    h = _div(lax.broadcasted_iota(jnp.int32, (rows, 1), 0), rows_per_head)
    return jnp.exp2(-(8.0 / n_heads) * (h + 1).astype(F32))


def _softmax_step(s, v, m_ref, l_ref, acc_ref):
    kt = s.shape[1]
    m_prev = m_ref[...]
    m_next = jnp.maximum(m_prev, jnp.max(s, axis=1, keepdims=True))
    p = jnp.exp(s - _rep(m_next, kt))
    alpha = jnp.exp(m_prev - m_next)
    l_ref[...] = alpha * l_ref[...] + jnp.sum(p, axis=1, keepdims=True)
    m_ref[...] = m_next
    acc_ref[...] = acc_ref[...] * _rep(alpha, acc_ref.shape[1]) + _dot(p.astype(BF16), v)


def _softmax_init(m_ref, l_ref, acc_ref):
    m_ref[...] = jnp.full(m_ref.shape, M_INIT, F32)
    l_ref[...] = jnp.zeros(l_ref.shape, F32)
    acc_ref[...] = jnp.zeros(acc_ref.shape, F32)


def _softmax_out(l_ref, acc_ref):
    inv = 1.0 / jnp.maximum(l_ref[...], 1e-30)
    return acc_ref[...] * _rep(inv, acc_ref.shape[1])


def _topk_mask(imp, valid, k):
    n = imp.shape[1]
    lane = lax.broadcasted_iota(jnp.int32, imp.shape, 1).astype(F32)
    work = jnp.where(valid, imp, -1.0)
    sel = jnp.zeros(imp.shape, F32)
    for _ in range(min(k, n)):
        mx = jnp.max(work, axis=1, keepdims=True)
        idx = jnp.min(jnp.where(work == mx, lane, float(n)), axis=1, keepdims=True)
        hit = lane == idx
        sel = jnp.where(hit & (mx >= 0.0), 1.0, sel)
        work = jnp.where(hit, -2.0, work)
    return sel


def _ffn_kernel(x_ref, g_ref, wg_ref, wu_ref, wo_ref, fg_ref, o_ref, xn_ref, acc_ref, *, final_norm):
    j = pl.program_id(1)

    @pl.when(j == 0)
    def _():
        xn_ref[...] = _rms(x_ref[...], g_ref[...]).astype(BF16)
        acc_ref[...] = jnp.zeros(acc_ref.shape, F32)

    xn = xn_ref[...]
    gate = _dot(xn, wg_ref[...])
    up = _dot(xn, wu_ref[...])
    h = (gate * (1.0 / (1.0 + jnp.exp(-gate))) * up).astype(BF16)
    acc_ref[...] += _dot(h, wo_ref[...])

    @pl.when(j == pl.num_programs(1) - 1)
    def _():
        y = x_ref[...] + 0.5 * acc_ref[...]
        if final_norm:
            y = _rms(y, fg_ref[...])
        o_ref[...] = y


def _ffn(x, g, wi, wo, fg, *, final_norm, tm, tf):
    t, d = x.shape
    dff = wo.shape[0]
    nj = dff // tf
    return pl.pallas_call(
        functools.partial(_ffn_kernel, final_norm=final_norm),
        grid=(t // tm, nj),
        in_specs=[
            pl.BlockSpec((tm, d), lambda i, j: (i, 0)),
            pl.BlockSpec((1, d), lambda i, j: (0, 0)),
            pl.BlockSpec((d, tf), lambda i, j: (0, j)),
            pl.BlockSpec((d, tf), lambda i, j: (0, j + nj)),
            pl.BlockSpec((tf, d), lambda i, j: (j, 0)),
            pl.BlockSpec((1, d), lambda i, j: (0, 0)),
        ],
        out_specs=pl.BlockSpec((tm, d), lambda i, j: (i, 0)),
        out_shape=jax.ShapeDtypeStruct((t, d), F32),
        scratch_shapes=[pltpu.VMEM((tm, d), BF16), pltpu.VMEM((tm, d), F32)],
        compiler_params=_cparams(("parallel", "arbitrary")),
        name="ffn",
    )(x, g, wi, wi, wo, fg)


def _mm_kernel(*refs, norm, residual):
    refs = list(refs)
    a_ref = refs.pop(0)
    g_ref = refs.pop(0) if norm else None
    w_ref = refs.pop(0)
    r_ref = refs.pop(0) if residual else None
    o_ref = refs.pop(0)
    a = a_ref[...]
    if norm:
        a = _rms(a, g_ref[...])
    y = _dot(a.astype(BF16), w_ref[...])
    if residual:
        y = y + r_ref[...]
    o_ref[...] = y


def _matmul(a, w, *, g=None, res=None, tm, tn, name):
    t, k = a.shape
    n = w.shape[1]
    args, specs = [a], [pl.BlockSpec((tm, k), lambda i, j: (i, 0))]
    if g is not None:
        args.append(g)
        specs.append(pl.BlockSpec((1, k), lambda i, j: (0, 0)))
    args.append(w)
    specs.append(pl.BlockSpec((k, tn), lambda i, j: (0, j)))
    if res is not None:
        args.append(res)
        specs.append(pl.BlockSpec((tm, tn), lambda i, j: (i, j)))
    return pl.pallas_call(
        functools.partial(_mm_kernel, norm=g is not None, residual=res is not None),
        grid=(t // tm, n // tn),
        in_specs=specs,
        out_specs=pl.BlockSpec((tm, tn), lambda i, j: (i, j)),
        out_shape=jax.ShapeDtypeStruct((t, n), F32),
        compiler_params=_cparams(("parallel", "parallel")),
        name=name,
    )(*args)


def _proj_kernel(x_ref, g_ref, w_ref, qg_ref, wn_ref, wr_ref, wrr_ref, wuk_ref, kvg_ref, cos_ref,
                 sin_ref, wp_ref,
                 qmla_ref, qd_ref, qn_ref, lat_ref, latk_ref, dkv_ref, dkvb_ref, cmp_ref, sel_ref,
                 selb_ref, win_ref, winb_ref, gates_ref, pool_ref):
    tm = x_ref.shape[0]
    qdt = qmla_ref.dtype
    xn = _rms(x_ref[...], g_ref[...]).astype(BF16)

    def seg(a, n):
        return _dot(xn, w_ref[:, a:a + n])

    cos_t = cos_ref[...]
    sin_t = sin_ref[...]
    qcn = _rms(seg(S_QC, A_QRANK), qg_ref[...]).astype(BF16)
    sc_a = (A_NOPE + A_ROPE) ** -0.5
    for h in range(A_HEADS):
        cols = slice(h * LANE, (h + 1) * LANE)
        q_nope = _dot(qcn, wn_ref[:, cols]).astype(BF16)
        q_lat = _dot(q_nope, wuk_ref[h])
        q_rope = _dot(qcn, wr_ref[:, cols]) * cos_t + _dot(qcn, wrr_ref[:, cols]) * sin_t
        qmla_ref[h, :, 0:LANE] = (q_lat * sc_a).astype(qdt)
        qmla_ref[h, :, LANE:2 * LANE] = (q_rope * sc_a).astype(qdt)
    c = _rms(seg(S_CKV, A_KVRANK), kvg_ref[...])
    kr = seg(S_KR, LANE) * cos_t + seg(S_KRR, LANE) * sin_t
    lat_ref[:, 0:A_KVRANK] = c
    lat_ref[:, A_KVRANK:A_LAT] = kr[:, 0:A_ROPE]
    latk_ref[:, 0:LANE] = c.astype(BF16)
    latk_ref[:, LANE:2 * LANE] = kr.astype(BF16)
    sc_b = B_D ** -0.5
    for t in range(2 * B_HEADS):
        qd_ref[t] = (seg(S_QB + t * LANE, LANE) * sc_b).astype(qdt)
    dkv = seg(S_DKV, 4 * B_D)
    dkv_ref[...] = dkv
    dkvb_ref[...] = dkv.astype(BF16)
    sc_c = C_DK ** -0.5
    for h in range(C_HEADS):
        qn_ref[h] = (seg(S_QN + h * LANE, LANE) * sc_c).astype(qdt)
    cmp_r = seg(S_CMP, LANE)
    cmp_ref[...] = cmp_r
    pool_ref[0] = jnp.sum(cmp_r.reshape(tm // CMP_BLOCK, CMP_BLOCK, LANE) * wp_ref[...][None], axis=1)
    sel_r = seg(S_SEL, LANE)
    sel_ref[...] = sel_r
    selb_ref[...] = sel_r.astype(BF16)
    win_r = seg(S_WIN, LANE)
    win_ref[...] = win_r
    winb_ref[...] = win_r.astype(BF16)
    gl = seg(S_GL, LANE)
    gates_ref[...] = 1.0 / (1.0 + jnp.exp(-gl))


def _project(x, lw, cos_t, sin_t, *, tm, table_blocks, qdt):
    t, d = x.shape
    nt = t // tm
    const = lambda i: (0, 0)
    row = lambda i: (i, 0)
    one = pl.Buffered(1)
    in_specs = [
        pl.BlockSpec((tm, d), row),
        pl.BlockSpec((1, d), const),
        pl.BlockSpec((d, N_WALL), const, pipeline_mode=one),
        pl.BlockSpec((1, A_QRANK), const),
        pl.BlockSpec((A_QRANK, A_HEADS * LANE), const, pipeline_mode=one),
        pl.BlockSpec((A_QRANK, A_HEADS * LANE), const, pipeline_mode=one),
        pl.BlockSpec((A_QRANK, A_HEADS * LANE), const, pipeline_mode=one),
        pl.BlockSpec((A_HEADS, LANE, LANE), lambda i: (0, 0, 0), pipeline_mode=one),
        pl.BlockSpec((1, A_KVRANK), const),
        pl.BlockSpec((tm, LANE), lambda i: (i % table_blocks, 0)),
        pl.BlockSpec((tm, LANE), lambda i: (i % table_blocks, 0)),
        pl.BlockSpec((CMP_BLOCK, LANE), const),
    ]
    hrow = lambda i: (0, i, 0)
    outs = [
        (jax.ShapeDtypeStruct((A_HEADS, t, 2 * LANE), qdt), pl.BlockSpec((A_HEADS, tm, 2 * LANE), hrow)),
        (jax.ShapeDtypeStruct((2 * B_HEADS, t, LANE), qdt), pl.BlockSpec((2 * B_HEADS, tm, LANE), hrow)),
        (jax.ShapeDtypeStruct((C_HEADS, t, LANE), qdt), pl.BlockSpec((C_HEADS, tm, LANE), hrow)),
        (jax.ShapeDtypeStruct((t, A_LAT), F32), pl.BlockSpec((tm, A_LAT), row)),
        (jax.ShapeDtypeStruct((t, 2 * LANE), BF16), pl.BlockSpec((tm, 2 * LANE), row)),
        (jax.ShapeDtypeStruct((t, 4 * B_D), F32), pl.BlockSpec((tm, 4 * B_D), row)),
        (jax.ShapeDtypeStruct((t, 4 * B_D), BF16), pl.BlockSpec((tm, 4 * B_D), row)),
        (jax.ShapeDtypeStruct((t, LANE), F32), pl.BlockSpec((tm, LANE), row)),
        (jax.ShapeDtypeStruct((t, LANE), F32), pl.BlockSpec((tm, LANE), row)),
        (jax.ShapeDtypeStruct((t, LANE), BF16), pl.BlockSpec((tm, LANE), row)),
        (jax.ShapeDtypeStruct((t, LANE), F32), pl.BlockSpec((tm, LANE), row)),
        (jax.ShapeDtypeStruct((t, LANE), BF16), pl.BlockSpec((tm, LANE), row)),
        (jax.ShapeDtypeStruct((t, LANE), F32), pl.BlockSpec((tm, LANE), row)),
        (jax.ShapeDtypeStruct((nt, tm // CMP_BLOCK, LANE), F32),
         pl.BlockSpec((1, tm // CMP_BLOCK, LANE), lambda i: (i, 0, 0))),
    ]
    res = pl.pallas_call(
        _proj_kernel,
        grid=(nt,),
        in_specs=in_specs,
        out_specs=[o[1] for o in outs],
        out_shape=[o[0] for o in outs],
        compiler_params=_cparams(("parallel",)),
        name="project",
    )(x, lw["mix_norm"], lw["w_all"], lw["q_norm"], lw["wn"], lw["wr"], lw["wrr"], lw["wuk"],
      lw["kv_norm"], cos_t, sin_t, lw["w_pool"])
    keys = ("qmla", "qd", "qn", "lat", "latk", "dkv", "dkvb", "cmp", "sel", "selb", "win", "winb",
            "gates", "pool")
    out = dict(zip(keys, res))
    out["pool"] = out["pool"].reshape(t // CMP_BLOCK, LANE)
    return out


def _prompt_attn_kernel(qmla_ref, qd_ref, qn_ref, latk_ref, dkvb_ref, selb_ref, winb_ref, pool_ref,
                        gates_ref, wuv_ref, dng_ref, lam_ref, exp_ref, o_ref,
                        ma_ref, la_ref, acca_ref, mb_ref, lb_ref, accb_ref, mc_ref, lc_ref, accc_ref,
                        *, kt, lam_init):
    i = pl.program_id(1)
    qb = qmla_ref.shape[1]
    seq = latk_ref.shape[0]
    nblk = pool_ref.shape[0]
    ra, rb, rc = A_HEADS * qb, 2 * B_HEADS * qb, C_HEADS * qb
    q0 = i * qb
    qa = qmla_ref[...].reshape(ra, 2 * LANE)
    qd = qd_ref[...].reshape(rb, LANE)
    qn = qn_ref[...].reshape(rc, LANE)

    def qpos_rows(rows, width):
        return q0 + _mod(lax.broadcasted_iota(jnp.int32, (rows, width), 0), qb)

    slope_b = _alibi_col(rb, 2 * qb, B_HEADS)
    slope_c = _alibi_col(rc, qb, C_HEADS)

    kcv = pool_ref[...].astype(BF16)
    blk = lax.broadcasted_iota(jnp.int32, (rc, nblk), 1)
    qp = qpos_rows(rc, nblk)
    cvalid = blk < _div(qp, CMP_BLOCK)
    cdist = (qp - ((blk + 1) * CMP_BLOCK - 1)).astype(F32)
    sc = jnp.where(cvalid, _dot_t(qn, kcv) - slope_c * cdist, MASKED)
    mcmp = jnp.max(sc, axis=1, keepdims=True)
    pc = jnp.where(cvalid, jnp.exp(sc - mcmp), 0.0)
    pc = pc / jnp.maximum(jnp.sum(pc, axis=1, keepdims=True), 1e-30)
    o_cmp = _dot(pc.astype(BF16), kcv)
    imp = jnp.sum(pc.reshape(C_HEADS, qb, nblk), axis=0)
    sel = _topk_mask(imp, cvalid[:qb], N_SEL).astype(BF16)

    wlen = WINDOW + qb
    w0 = pl.multiple_of(jnp.maximum(q0 - WINDOW, 0), qb)
    wrows = winb_ref[pl.ds(w0, wlen), :]
    wd = qpos_rows(rc, wlen) - (w0 + lax.broadcasted_iota(jnp.int32, (rc, wlen), 1))
    sw = jnp.where((wd >= 0) & (wd <= WINDOW), _dot_t(qn, wrows) - slope_c * wd.astype(F32), MASKED)
    pw = jnp.exp(sw - jnp.max(sw, axis=1, keepdims=True))
    o_win = _dot(pw.astype(BF16), wrows) / jnp.sum(pw, axis=1, keepdims=True)

    _softmax_init(ma_ref, la_ref, acca_ref)
    _softmax_init(mb_ref, lb_ref, accb_ref)
    _softmax_init(mc_ref, lc_ref, accc_ref)

    def body(j, carry):
        k0 = pl.multiple_of(j * kt, kt)
        kpos_a = k0 + lax.broadcasted_iota(jnp.int32, (ra, kt), 1)
        lat = latk_ref[pl.ds(k0, kt), :]
        sa = jnp.where(kpos_a <= qpos_rows(ra, kt), _dot_t(qa, lat), MASKED)
        _softmax_step(sa, lat[:, 0:LANE], ma_ref, la_ref, acca_ref)

        dk = dkvb_ref[pl.ds(k0, kt), :]
        dist_b = qpos_rows(rb, kt) - (k0 + lax.broadcasted_iota(jnp.int32, (rb, kt), 1))
        sb = jnp.where(dist_b >= 0, _dot_t(qd, dk[:, 0:LANE]) - slope_b * dist_b.astype(F32), MASKED)
        _softmax_step(sb, dk[:, LANE:2 * LANE], mb_ref, lb_ref, accb_ref)

        ks = selb_ref[pl.ds(k0, kt), :]
        qp1 = q0 + lax.broadcasted_iota(jnp.int32, (qb, kt), 0)
        kp1 = k0 + lax.broadcasted_iota(jnp.int32, (qb, kt), 1)
        chosen = _dot(sel, exp_ref[j]) > 0.5
        allowed = chosen | ((_div(kp1, CMP_BLOCK) == _div(qp1, CMP_BLOCK)) & (kp1 <= qp1))
        dist_c = (qp1 - kp1).astype(F32)
        ss = _dot_t(qn, ks).reshape(C_HEADS, qb, kt) - slope_c.reshape(C_HEADS, qb, 1) * dist_c[None]
        ss = jnp.where(allowed[None], ss, MASKED).reshape(rc, kt)
        _softmax_step(ss, ks, mc_ref, lc_ref, accc_ref)
        return carry

    lax.fori_loop(0, (q0 + qb - 1) // kt + 1, body, 0)

    o_lat = _softmax_out(la_ref, acca_ref).astype(BF16).reshape(A_HEADS, qb, LANE)
    for p in range(A_HEADS // 2):
        pair = jnp.concatenate([o_lat[2 * p], o_lat[2 * p + 1]], axis=-1)
        o_ref[:, p * LANE:(p + 1) * LANE] = _dot(pair, wuv_ref[p]).astype(o_ref.dtype)
    o_dif = _softmax_out(lb_ref, accb_ref).reshape(2 * B_HEADS, qb, LANE)
    lam = lam_ref[...]
    for h in range(B_HEADS):
        od = o_dif[2 * h] - lam * o_dif[2 * h + 1]
        od = _rms(od, dng_ref[...]) * (1.0 - lam_init)
        o_ref[:, MIX_A + h * LANE:MIX_A + (h + 1) * LANE] = od.astype(o_ref.dtype)
    o_sel = _softmax_out(lc_ref, accc_ref).reshape(C_HEADS, qb, LANE)
    o_cmp = o_cmp.reshape(C_HEADS, qb, LANE)
    o_win = o_win.reshape(C_HEADS, qb, LANE)
    gates = gates_ref[...]
    for h in range(C_HEADS):
        on = (gates[:, 3 * h:3 * h + 1] * o_cmp[h] + gates[:, 3 * h + 1:3 * h + 2] * o_sel[h]
              + gates[:, 3 * h + 2:3 * h + 3] * o_win[h])
        o_ref[:, MIX_A + MIX_B + h * LANE:MIX_A + MIX_B + (h + 1) * LANE] = on.astype(o_ref.dtype)


def _prompt_attn(pr, lw, lam_row, expand, *, batch, seq, lam_init, qb=128, kt=512):
    kt = min(kt, seq)
    nq = seq // qb
    nblk = seq // CMP_BLOCK
    hq = lambda b, i: (0, b * nq + i, 0)
    perb = lambda b, i: (b, 0)
    const2 = lambda b, i: (0, 0)
    const3 = lambda b, i: (0, 0, 0)
    ra, rb, rc = A_HEADS * qb, 2 * B_HEADS * qb, C_HEADS * qb
    stat = lambda r: [pltpu.VMEM((r, LANE), F32)] * 3
    return pl.pallas_call(
        functools.partial(_prompt_attn_kernel, kt=kt, lam_init=lam_init),
        grid=(batch, nq),
        in_specs=[
            pl.BlockSpec((A_HEADS, qb, 2 * LANE), hq),
            pl.BlockSpec((2 * B_HEADS, qb, LANE), hq),
            pl.BlockSpec((C_HEADS, qb, LANE), hq),
            pl.BlockSpec((seq, 2 * LANE), perb),
            pl.BlockSpec((seq, 4 * B_D), perb),
            pl.BlockSpec((seq, LANE), perb),
            pl.BlockSpec((seq, LANE), perb),
            pl.BlockSpec((nblk, LANE), perb),
            pl.BlockSpec((qb, LANE), lambda b, i: (b * nq + i, 0)),
            pl.BlockSpec((A_HEADS // 2, 2 * LANE, LANE), const3),
            pl.BlockSpec((1, LANE), const2),
            pl.BlockSpec((1, LANE), const2),
            pl.BlockSpec((seq // kt, nblk, kt), const3),
        ],
        out_specs=pl.BlockSpec((qb, MIX_W), lambda b, i: (b * nq + i, 0)),
        out_shape=jax.ShapeDtypeStruct((batch * seq, MIX_W), BF16),
        scratch_shapes=stat(ra) + stat(rb) + stat(rc),
        compiler_params=_cparams(("parallel", "arbitrary")),
        name="prompt_attn",
    )(pr["qmla"], pr["qd"], pr["qn"], pr["latk"], pr["dkvb"], pr["selb"], pr["winb"], pr["pool"],
      pr["gates"], lw["wuv_pair"], lw["diff_norm"], lam_row, expand)


def _page_copy(cache_ref, layer, page, dst, sem):
    return pltpu.make_async_copy(cache_ref.at[layer, page], dst, sem)


def _stream_pages(cache_ref, layer, pt_ref, buf_ref, sem_ref, n_chunks, ppc, consume):
    def start(c, slot):
        for p in range(ppc):
            _page_copy(cache_ref, layer, pt_ref[0, 0, c * ppc + p], buf_ref.at[slot, p], sem_ref.at[slot]).start()

    def wait(slot):
        for p in range(ppc):
            _page_copy(cache_ref, layer, 0, buf_ref.at[slot, p], sem_ref.at[slot]).wait()

    start(0, 0)

    def body(c, carry):
        slot = c % 2

        @pl.when(c + 1 < n_chunks)
        def _():
            start(c + 1, 1 - slot)

        wait(slot)
        consume(c, buf_ref.at[slot])
        return carry

    lax.fori_loop(0, n_chunks, body, 0)


def _new_rows_tile(pad_ref, new_ref):
    pad_ref[...] = jnp.zeros(pad_ref.shape, F32)
    pad_ref[0:new_ref.shape[0], :] = new_ref[...]
    return pad_ref[...].astype(BF16)


def _dec_mla_kernel(pt_ref, q_ref, new_ref, wuv_ref, cache_ref, o_ref,
                    buf_ref, sem_ref, pad_ref, m_ref, l_ref, acc_ref, *, layer, ppc):
    t = q_ref.shape[1]
    rows = A_HEADS * t
    n_pages = pt_ref.shape[2]
    q = q_ref[...].reshape(rows, 2 * LANE)[:, 0:A_LAT].astype(BF16)
    _softmax_init(m_ref, l_ref, acc_ref)

    def consume(c, chunk):
        k = chunk[...].reshape(ppc * PAGE_SIZE, A_LAT).astype(BF16)
        _softmax_step(_dot_t(q, k), k[:, 0:A_KVRANK], m_ref, l_ref, acc_ref)

    _stream_pages(cache_ref, layer, pt_ref, buf_ref, sem_ref, n_pages // ppc, ppc, consume)
    kn = _new_rows_tile(pad_ref, new_ref)
    qj = _mod(lax.broadcasted_iota(jnp.int32, (rows, PAGE_SIZE), 0), t)
    kj = lax.broadcasted_iota(jnp.int32, (rows, PAGE_SIZE), 1)
    sn = jnp.where(kj <= qj, _dot_t(q, kn), MASKED)
    _softmax_step(sn, kn[:, 0:A_KVRANK], m_ref, l_ref, acc_ref)
    o_lat = _softmax_out(l_ref, acc_ref).astype(BF16)
    full = _dot(o_lat, wuv_ref[...]).reshape(A_HEADS, t, A_HEADS * A_VDIM)
    own = (_div(lax.broadcasted_iota(jnp.int32, (A_HEADS, 1, A_HEADS * A_VDIM), 2), A_VDIM)
           == lax.broadcasted_iota(jnp.int32, (A_HEADS, 1, A_HEADS * A_VDIM), 0))
    o_ref[...] = jnp.sum(jnp.where(own, full, 0.0), axis=0)


def _dec_diff_kernel(pt_ref, q_ref, new_ref, dng_ref, lam_ref, cache_ref, o_ref,
                     buf_ref, sem_ref, pad_ref, m_ref, l_ref, acc_ref, *, layer, ppc, past, lam_init):
    t = q_ref.shape[1]
    rows = 2 * B_HEADS * t
    n_pages = pt_ref.shape[2]
    q = q_ref[...].reshape(rows, LANE).astype(BF16)
    slope = _alibi_col(rows, 2 * t, B_HEADS)
    _softmax_init(m_ref, l_ref, acc_ref)
    ck = ppc * PAGE_SIZE
    qpos = past + _mod(lax.broadcasted_iota(jnp.int32, (rows, ck), 0), t)

    def consume(c, chunk):
        k = chunk[...].reshape(ck, 4 * B_D).astype(BF16)
        dist = (qpos - (c * ck + lax.broadcasted_iota(jnp.int32, (rows, ck), 1))).astype(F32)
        _softmax_step(_dot_t(q, k[:, 0:LANE]) - slope * dist, k[:, LANE:2 * LANE], m_ref, l_ref, acc_ref)

    _stream_pages(cache_ref, layer, pt_ref, buf_ref, sem_ref, n_pages // ppc, ppc, consume)
    kn = _new_rows_tile(pad_ref, new_ref)
    dj = (_mod(lax.broadcasted_iota(jnp.int32, (rows, PAGE_SIZE), 0), t)
          - lax.broadcasted_iota(jnp.int32, (rows, PAGE_SIZE), 1))
    sn = jnp.where(dj >= 0, _dot_t(q, kn[:, 0:LANE]) - slope * dj.astype(F32), MASKED)
    _softmax_step(sn, kn[:, LANE:2 * LANE], m_ref, l_ref, acc_ref)
    o = _softmax_out(l_ref, acc_ref)
    lam = lam_ref[...]
    for h in range(B_HEADS):
        od = o[2 * h * t:(2 * h + 1) * t] - lam * o[(2 * h + 1) * t:(2 * h + 2) * t]
        o_ref[:, h * LANE:(h + 1) * LANE] = _rms(od, dng_ref[...]) * (1.0 - lam_init)


def _dec_nsa_kernel(pt_ref, q_ref, seln_ref, winn_ref, wst_ref, gates_ref, wp_ref, exp_ref,
                    cmp_cache_ref, sel_cache_ref, o_ref, nwin_ref,
                    buf_ref, sem_ref, pad_ref, pool_ref, selrows_ref, wbuf_ref, m_ref, l_ref, acc_ref,
                    *, layer, ppc, past):
    t = q_ref.shape[1]
    rows = C_HEADS * t
    n_pages = pt_ref.shape[2]
    n_chunks = n_pages // ppc
    ck = ppc * PAGE_SIZE
    bpc = ck // CMP_BLOCK
    nblk = n_chunks * bpc
    q = q_ref[...].reshape(rows, LANE).astype(BF16)
    slope = _alibi_col(rows, t, C_HEADS)

    def pool(c, chunk):
        r = chunk[...].reshape(bpc, CMP_BLOCK, LANE) * wp_ref[...][None]
        pool_ref[pl.ds(pl.multiple_of(c * bpc, bpc), bpc), :] = jnp.sum(r, axis=1)

    _stream_pages(cmp_cache_ref, layer, pt_ref, buf_ref, sem_ref, n_chunks, ppc, pool)
    kcv = pool_ref[...].astype(BF16)
    blk = lax.broadcasted_iota(jnp.int32, (rows, nblk), 1)
    qp = past + _mod(lax.broadcasted_iota(jnp.int32, (rows, nblk), 0), t)
    cvalid = blk < _div(qp, CMP_BLOCK)
    cdist = (qp - ((blk + 1) * CMP_BLOCK - 1)).astype(F32)
    sc = jnp.where(cvalid, _dot_t(q, kcv) - slope * cdist, MASKED)
    pc = jnp.where(cvalid, jnp.exp(sc - jnp.max(sc, axis=1, keepdims=True)), 0.0)
    pc = pc / jnp.maximum(jnp.sum(pc, axis=1, keepdims=True), 1e-30)
    o_cmp = _dot(pc.astype(BF16), kcv)
    imp = jnp.sum(pc.reshape(C_HEADS, t, nblk), axis=0)
    sel = _topk_mask(imp, cvalid[:t], N_SEL)
    sel_rows = jnp.tile(sel, (C_HEADS, 1))
    for c in range(n_chunks):
        selrows_ref[c] = sel_rows[:, c * bpc:(c + 1) * bpc]

    _softmax_init(m_ref, l_ref, acc_ref)
    qpos = past + _mod(lax.broadcasted_iota(jnp.int32, (rows, ck), 0), t)

    def sweep(c, chunk):
        k = chunk[...].reshape(ck, LANE).astype(BF16)
        chosen = _dot(selrows_ref[c].astype(BF16), exp_ref[...]) > 0.5
        dist = (qpos - (c * ck + lax.broadcasted_iota(jnp.int32, (rows, ck), 1))).astype(F32)
        s = jnp.where(chosen, _dot_t(q, k) - slope * dist, MASKED)
        _softmax_step(s, k, m_ref, l_ref, acc_ref)

    _stream_pages(sel_cache_ref, layer, pt_ref, buf_ref, sem_ref, n_chunks, ppc, sweep)
    kn = _new_rows_tile(pad_ref, seln_ref)
    dj = (_mod(lax.broadcasted_iota(jnp.int32, (rows, PAGE_SIZE), 0), t)
          - lax.broadcasted_iota(jnp.int32, (rows, PAGE_SIZE), 1))
    sn = jnp.where(dj >= 0, _dot_t(q, kn) - slope * dj.astype(F32), MASKED)
    _softmax_step(sn, kn, m_ref, l_ref, acc_ref)
    o_sel = _softmax_out(l_ref, acc_ref)

    wb = wst_ref.shape[0]
    wbuf_ref[...] = jnp.zeros(wbuf_ref.shape, F32)
    wbuf_ref[0:wb, :] = wst_ref[...]
    wbuf_ref[wb:wb + t, :] = winn_ref[...]
    wrows = wbuf_ref[...].astype(BF16)
    wl = wbuf_ref.shape[0]
    idx = lax.broadcasted_iota(jnp.int32, (rows, wl), 1)
    wpos = jnp.where(idx < wb, past - wb + idx, past + idx - wb)
    wd = past + _mod(lax.broadcasted_iota(jnp.int32, (rows, wl), 0), t) - wpos
    ok = (wd >= 0) & (wd <= WINDOW) & (idx < wb + t) & (wpos >= 0)
    sw = jnp.where(ok, _dot_t(q, wrows) - slope * wd.astype(F32), MASKED)
    pw = jnp.exp(sw - jnp.max(sw, axis=1, keepdims=True))
    o_win = _dot(pw.astype(BF16), wrows) / jnp.sum(pw, axis=1, keepdims=True)
    nwin_ref[0:wb - t, :] = wst_ref[t:wb, :]
    nwin_ref[wb - t:wb, :] = winn_ref[...]

    gates = gates_ref[...]
    for h in range(C_HEADS):
        r = slice(h * t, (h + 1) * t)
        o_ref[:, h * LANE:(h + 1) * LANE] = (gates[:, 3 * h:3 * h + 1] * o_cmp[r]
                                              + gates[:, 3 * h + 1:3 * h + 2] * o_sel[r]
                                              + gates[:, 3 * h + 2:3 * h + 3] * o_win[r])


def _sample_attn(sp, lw, lam_row, expand_s, caches, win_state, page_table3, *, layer, lam_init, past,
                 ppc=16):
    cache_mla, cache_diff, cache_cmp, cache_sel = caches
    db, _, n_pages = page_table3.shape
    ts = sp["lat"].shape[0]
    t = ts // db
    ppc = min(ppc, n_pages)
    wb = win_state.shape[2]
    pt_spec = pl.BlockSpec((1, 1, n_pages), lambda b: (b, 0, 0), memory_space=pltpu.SMEM)
    hq = lambda b: (0, b, 0)
    row = lambda b: (b, 0)
    const2 = lambda b: (0, 0)
    any_spec = pl.BlockSpec(memory_space=pl.ANY)
    stat = lambda r: [pltpu.VMEM((r, LANE), F32)] * 3

    def stream_scratch(width):
        return [pltpu.VMEM((2, ppc, PAGE_SIZE, width), F32), pltpu.SemaphoreType.DMA((2,)),
                pltpu.VMEM((PAGE_SIZE, width), F32)]

    o_a = pl.pallas_call(
        functools.partial(_dec_mla_kernel, layer=layer, ppc=ppc),
        grid=(db,),
        in_specs=[pt_spec,
                  pl.BlockSpec((A_HEADS, t, 2 * LANE), hq),
                  pl.BlockSpec((t, A_LAT), row),
                  pl.BlockSpec((A_KVRANK, A_HEADS * A_VDIM), const2),
                  any_spec],
        out_specs=pl.BlockSpec((t, MIX_A), row),
        out_shape=jax.ShapeDtypeStruct((ts, MIX_A), F32),
        scratch_shapes=stream_scratch(A_LAT) + stat(A_HEADS * t),
        compiler_params=_cparams(("arbitrary",)),
        name="sample_mla",
    )(page_table3, sp["qmla"], sp["lat"], lw["wuv_flat"], cache_mla)

    o_b = pl.pallas_call(
        functools.partial(_dec_diff_kernel, layer=layer, ppc=ppc, past=past, lam_init=lam_init),
        grid=(db,),
        in_specs=[pt_spec,
                  pl.BlockSpec((2 * B_HEADS, t, LANE), hq),
                  pl.BlockSpec((t, 4 * B_D), row),
                  pl.BlockSpec((1, LANE), const2),
                  pl.BlockSpec((1, LANE), const2),
                  any_spec],
        out_specs=pl.BlockSpec((t, MIX_B), row),
        out_shape=jax.ShapeDtypeStruct((ts, MIX_B), F32),
        scratch_shapes=stream_scratch(4 * B_D) + stat(2 * B_HEADS * t),
        compiler_params=_cparams(("arbitrary",)),
        name="sample_diff",
    )(page_table3, sp["qd"], sp["dkv"], lw["diff_norm"], lam_row, cache_diff)

    n_chunks = n_pages // ppc
    bpc = ppc * PAGE_SIZE // CMP_BLOCK
    rows_c = C_HEADS * t
    o_c, new_win = pl.pallas_call(
        functools.partial(_dec_nsa_kernel, layer=layer, ppc=ppc, past=past),
        grid=(db,),
        in_specs=[pt_spec,
                  pl.BlockSpec((C_HEADS, t, LANE), hq),
                  pl.BlockSpec((t, LANE), row),
                  pl.BlockSpec((t, LANE), row),
                  pl.BlockSpec((None, None, wb, LANE), lambda b: (layer, b, 0, 0)),
                  pl.BlockSpec((t, LANE), row),
                  pl.BlockSpec((CMP_BLOCK, LANE), const2),
                  pl.BlockSpec((bpc, ppc * PAGE_SIZE), const2),
                  any_spec, any_spec],
        out_specs=[pl.BlockSpec((t, MIX_C), row),
                   pl.BlockSpec((None, wb, LANE), lambda b: (b, 0, 0))],
        out_shape=[jax.ShapeDtypeStruct((ts, MIX_C), F32),
                   jax.ShapeDtypeStruct((db, wb, LANE), F32)],
        scratch_shapes=(stream_scratch(LANE)
                        + [pltpu.VMEM((n_chunks * bpc, LANE), F32),
                           pltpu.VMEM((n_chunks, rows_c, bpc), F32),
                           pltpu.VMEM((wb + PAGE_SIZE, LANE), F32)]
                        + stat(rows_c)),
        compiler_params=_cparams(("arbitrary",)),
        name="sample_nsa",
    )(page_table3, sp["qn"], sp["sel"], sp["win"], win_state, sp["gates"], lw["w_pool"],
      expand_s, cache_cmp, cache_sel)
    mixed = jnp.concatenate([o_a, o_b, o_c], axis=-1).astype(BF16)
    return mixed, new_win


def _xattn_kernel(x_ref, g_ref, wq_ref, mem_ref, wo_ref, o_ref, *, small):
    grp, tq, d = x_ref.shape
    x = x_ref[...].reshape(grp * tq, d)
    q = _dot(_rms(x, g_ref[...]).astype(BF16), wq_ref[...]) * (M_DH ** -0.5)
    hd = M_HEADS * M_DH
    outs = []
    for gi in range(grp):
        kv = mem_ref[gi]
        qg = q[gi * tq:(gi + 1) * tq]
        if not small:
            kv = kv.astype(BF16)
            qg = qg.astype(BF16)
        heads = []
        for h in range(M_HEADS):
            s = _dot_t(qg[:, h * M_DH:(h + 1) * M_DH], kv[:, h * M_DH:(h + 1) * M_DH])
            p = jnp.exp(s - jnp.max(s, axis=1, keepdims=True))
            inv = 1.0 / jnp.sum(p, axis=1, keepdims=True)
            if not small:
                p = p.astype(BF16)
            heads.append(_dot(p, kv[:, hd + h * M_DH:hd + (h + 1) * M_DH]) * inv)
        outs.append(jnp.concatenate(heads, axis=-1))
    attn = outs[0] if grp == 1 else jnp.concatenate(outs, axis=0)
    o_ref[...] = (x + _dot(attn.astype(BF16), wo_ref[...])).reshape(grp, tq, d)


def _xattn(x3, g, wq, mem, wo, *, layer, grp, tq):
    nb, t, d = x3.shape
    nm = mem.shape[-2]
    hd2 = mem.shape[-1]
    if mem.ndim == 4:
        mem_spec = pl.BlockSpec((None, grp, nm, hd2), lambda b, i: (layer, b, 0, 0))
    else:
        mem_spec = pl.BlockSpec((grp, nm, hd2), lambda b, i: (b, 0, 0))
    return pl.pallas_call(
        functools.partial(_xattn_kernel, small=tq < 16),
        grid=(nb // grp, t // tq),
        in_specs=[pl.BlockSpec((grp, tq, d), lambda b, i: (b, i, 0)),
                  pl.BlockSpec((1, d), lambda b, i: (0, 0)),
                  pl.BlockSpec((d, M_HEADS * M_DH), lambda b, i: (0, 0)),
                  mem_spec,
                  pl.BlockSpec((M_HEADS * M_DH, d), lambda b, i: (0, 0))],
        out_specs=pl.BlockSpec((grp, tq, d), lambda b, i: (b, i, 0)),
        out_shape=jax.ShapeDtypeStruct((nb, t, d), F32),
        compiler_params=_cparams(("parallel", "parallel")),
        name="xattn",
    )(x3, g, wq, mem, wo)


def _wall_columns():
    offs = np.concatenate([[0], np.cumsum(IN_SIZES)])
    o_qc, o_ckv, o_kr, o_qb, o_dkv, o_qn, o_cmp, o_sel, o_win, o_gl = offs[:-1]
    idx = np.zeros(N_WALL, np.int32)
    sgn = np.zeros(N_WALL, np.float32)

    def put(dst, src, sign=1.0):
        src = np.asarray(src)
        idx[dst:dst + len(src)] = src
        sgn[dst:dst + len(src)] = sign

    put(S_QC, o_qc + np.arange(A_QRANK))
    put(S_CKV, o_ckv + np.arange(A_KVRANK))
    for h in range(B_HEADS):
        put(S_QB + (2 * h) * LANE, o_qb + h * 2 * B_D + np.arange(B_D))
        put(S_QB + (2 * h + 1) * LANE + B_D, o_qb + h * 2 * B_D + B_D + np.arange(B_D))
    put(S_DKV, o_dkv + np.arange(4 * B_D))
    for h in range(C_HEADS):
        put(S_QN + h * LANE, o_qn + h * C_DK + np.arange(C_DK))
    put(S_CMP, o_cmp + np.arange(LANE))
    put(S_SEL, o_sel + np.arange(LANE))
    put(S_WIN, o_win + np.arange(LANE))
    half = A_ROPE // 2
    put(S_KR, o_kr + np.arange(A_ROPE))
    put(S_KRR, o_kr + half + np.arange(half), -1.0)
    put(S_KRR + half, o_kr + np.arange(half))
    put(S_GL, o_gl + np.arange(3 * C_HEADS))
    return idx, sgn


def _uq_columns():
    width = A_HEADS * LANE
    per = A_NOPE + A_ROPE
    half = A_ROPE // 2
    idx = np.zeros((3, width), np.int32)
    sgn = np.zeros((3, width), np.float32)
    for h in range(A_HEADS):
        idx[0, h * LANE:h * LANE + A_NOPE] = h * per + np.arange(A_NOPE)
        sgn[0, h * LANE:h * LANE + A_NOPE] = 1.0
        idx[1, h * LANE:h * LANE + A_ROPE] = h * per + A_NOPE + np.arange(A_ROPE)
        sgn[1, h * LANE:h * LANE + A_ROPE] = 1.0
        idx[2, h * LANE:h * LANE + half] = h * per + A_NOPE + half + np.arange(half)
        sgn[2, h * LANE:h * LANE + half] = -1.0
        idx[2, h * LANE + half:h * LANE + A_ROPE] = h * per + A_NOPE + np.arange(half)
        sgn[2, h * LANE + half:h * LANE + A_ROPE] = 1.0
    return idx, sgn


def _wout_rows():
    idx = np.zeros(MIX_W, np.int32)
    sgn = np.zeros(MIX_W, np.float32)
    idx[:MIX_A + MIX_B] = np.arange(MIX_A + MIX_B)
    sgn[:MIX_A + MIX_B] = 1.0
    for h in range(C_HEADS):
        dst = MIX_A + MIX_B + h * LANE + C_DK
        idx[dst:dst + C_DV] = MIX_A + MIX_B + h * C_DV + np.arange(C_DV)
        sgn[dst:dst + C_DV] = 1.0
    return idx, sgn


def _gather_cols(w, idx, sgn):
    return (jnp.take(w, jnp.asarray(idx), axis=1) * jnp.asarray(sgn)[None, :]).astype(BF16)


def _layer_weights(l, p):
    d = p["w_in"].shape[1]
    lw = {}
    lw["mix_norm"] = p["mix_norm"][l].reshape(1, d)
    lw["w_all"] = _gather_cols(p["w_in"][l], *_wall_columns())
    lw["q_norm"] = p["mla_q_norm"][l].reshape(1, A_QRANK)
    uidx, usgn = _uq_columns()
    lw["wn"], lw["wr"], lw["wrr"] = (_gather_cols(p["mla_w_uq"][l], uidx[k], usgn[k]) for k in range(3))
    wuk = jnp.transpose(p["mla_w_uk"][l], (1, 2, 0))
    lw["wuk"] = jnp.pad(wuk, ((0, 0), (0, LANE - A_NOPE), (0, 0))).astype(BF16)
    lw["kv_norm"] = p["mla_kv_norm"][l].reshape(1, A_KVRANK)
    wuv = p["mla_w_uv"][l]
    lw["wuv_flat"] = wuv.reshape(A_KVRANK, A_HEADS * A_VDIM).astype(BF16)
    z = jnp.zeros((A_KVRANK, A_VDIM), F32)
    pairs = [jnp.concatenate([jnp.concatenate([wuv[:, 2 * j], z], axis=1),
                              jnp.concatenate([z, wuv[:, 2 * j + 1]], axis=1)], axis=0)
             for j in range(A_HEADS // 2)]
    lw["wuv_pair"] = jnp.stack(pairs).astype(BF16)
    lw["diff_norm"] = p["diff_norm"][l].reshape(1, 2 * B_D)
    lw["w_pool"] = jnp.broadcast_to(p["nsa_w_cmp"][l][:, None], (CMP_BLOCK, LANE))
    ridx, rsgn = _wout_rows()
    lw["w_out"] = (jnp.take(p["w_out"][l], jnp.asarray(ridx), axis=0) * jnp.asarray(rsgn)[:, None]).astype(BF16)
    for k in ("ffn1_norm", "ffn2_norm", "xattn_norm", "mem_norm"):
        lw[k] = p[k][l].reshape(1, d)
    for k in ("ffn1_wi", "ffn1_wo", "ffn2_wi", "ffn2_wo", "xattn_wq", "xattn_wkv", "xattn_wo"):
        lw[k] = p[k][l].astype(BF16)
    return lw


def _rope_tables(pos):
    half = A_ROPE // 2
    inv = ROPE_BASE ** (-jnp.arange(half, dtype=F32) / half)
    ang = pos.astype(F32)[:, None] * inv
    pad = jnp.zeros((pos.shape[0], LANE - A_ROPE), F32)
    cos_t = jnp.concatenate([jnp.cos(ang), jnp.cos(ang), pad], axis=1)
    sin_t = jnp.concatenate([jnp.sin(ang), jnp.sin(ang), pad], axis=1)
    return cos_t, sin_t


def _block_expander(n_blocks, n_keys):
    e = (np.arange(n_keys)[None, :] // CMP_BLOCK) == np.arange(n_blocks)[:, None]
    return e.astype(np.float32)


def _tile(n, pref):
    t = min(n, pref)
    assert n % t == 0, (n, t)
    return t


def kernel(x_prompt, x_sample, mem_prompt, cache_mla, cache_diff, cache_nsa_cmp, cache_nsa_sel, state_nsa_win, cache_mem, page_table, ffn1_norm, ffn1_wi, ffn1_wo, mix_norm, w_in, mla_q_norm, mla_w_uq, mla_kv_norm, mla_w_uk, mla_w_uv, diff_lam, diff_norm, nsa_w_cmp, w_out, xattn_norm, mem_norm, xattn_wq, xattn_wkv, xattn_wo, ffn2_norm, ffn2_wi, ffn2_wo, final_norm):
    params = dict(ffn1_norm=ffn1_norm, ffn1_wi=ffn1_wi, ffn1_wo=ffn1_wo, mix_norm=mix_norm, w_in=w_in,
                  mla_q_norm=mla_q_norm, mla_w_uq=mla_w_uq, mla_kv_norm=mla_kv_norm, mla_w_uk=mla_w_uk,
                  mla_w_uv=mla_w_uv, diff_norm=diff_norm, nsa_w_cmp=nsa_w_cmp, w_out=w_out,
                  xattn_norm=xattn_norm, mem_norm=mem_norm, xattn_wq=xattn_wq, xattn_wkv=xattn_wkv,
                  xattn_wo=xattn_wo, ffn2_norm=ffn2_norm, ffn2_wi=ffn2_wi, ffn2_wo=ffn2_wo)
    batch, seq, d = x_prompt.shape
    db, ds, _ = x_sample.shape
    depth = w_in.shape[0]
    n_mem = mem_prompt.shape[1]
    n_pages = page_table.shape[1]
    past = n_pages * PAGE_SIZE
    tp, ts = batch * seq, db * ds
    assert ds % SUBLANE == 0 and seq % 128 == 0 and state_nsa_win.shape[2] == min(WINDOW, past)
    tm_p = _tile(seq, 512)
    tm_s = _tile(ts, 256)
    assert tm_s % ds == 0
    tf = _tile(ffn1_wo.shape[1], 512)
    tn = _tile(d, 1024)

    cos_p, sin_p = _rope_tables(jnp.arange(seq))
    cos_s, sin_s = _rope_tables(past + (jnp.arange(tm_s) % ds))
    kt = min(512, seq)
    expand_p = jnp.asarray(_block_expander(seq // CMP_BLOCK, seq).reshape(seq // CMP_BLOCK, seq // kt, kt)
                           .transpose(1, 0, 2), BF16)
    ppc = min(16, n_pages)
    expand_s = jnp.asarray(_block_expander(ppc * PAGE_SIZE // CMP_BLOCK, ppc * PAGE_SIZE), BF16)
    page_table3 = page_table.reshape(db, 1, n_pages)
    fg = final_norm.reshape(1, d)

    xp = x_prompt.reshape(tp, d)
    xs = x_sample.reshape(ts, d)
    mem2 = mem_prompt.reshape(batch * n_mem, d)
    outs = {k: [] for k in ("mla_p", "mla_s", "diff_p", "diff_s", "cmp_p", "cmp_s", "sel_p", "sel_s",
                            "win_p", "win_s", "mem_p")}
    for l in range(depth):
        lw = _layer_weights(l, params)
        lam_init = 0.8 - 0.6 * math.exp(-0.3 * l)
        dl = diff_lam[l].astype(F32)
        lam = jnp.exp(jnp.sum(dl[0] * dl[1])) - jnp.exp(jnp.sum(dl[2] * dl[3])) + lam_init
        lam_row = jnp.full((1, LANE), lam, F32)
        xp = _ffn(xp, lw["ffn1_norm"], lw["ffn1_wi"], lw["ffn1_wo"], fg, final_norm=False, tm=tm_p, tf=tf)
        xs = _ffn(xs, lw["ffn1_norm"], lw["ffn1_wi"], lw["ffn1_wo"], fg, final_norm=False, tm=tm_s, tf=tf)
        pr = _project(xp, lw, cos_p, sin_p, tm=tm_p, table_blocks=seq // tm_p, qdt=BF16)
        mix_p = _prompt_attn(pr, lw, lam_row, expand_p, batch=batch, seq=seq, lam_init=lam_init, kt=kt)
        xp = _matmul(mix_p, lw["w_out"], res=xp, tm=tm_p, tn=tn, name="w_out")
        sp = _project(xs, lw, cos_s, sin_s, tm=tm_s, table_blocks=1, qdt=F32)
        mix_s, new_win = _sample_attn(sp, lw, lam_row, expand_s,
                                      (cache_mla, cache_diff, cache_nsa_cmp, cache_nsa_sel),
                                      state_nsa_win, page_table3, layer=l, lam_init=lam_init, past=past,
                                      ppc=ppc)
        xs = _matmul(mix_s, lw["w_out"], res=xs, tm=tm_s, tn=tn, name="w_out_s")
        mem_kv = _matmul(mem2, lw["xattn_wkv"], g=lw["mem_norm"], tm=_tile(batch * n_mem, 512),
                         tn=_tile(lw["xattn_wkv"].shape[1], 1024), name="mem_kv")
        mem_kv = mem_kv.reshape(batch, n_mem, -1)
        xp = _xattn(xp.reshape(batch, seq, d), lw["xattn_norm"], lw["xattn_wq"], mem_kv, lw["xattn_wo"],
                    layer=l, grp=1, tq=tm_p).reshape(tp, d)
        xs = _xattn(xs.reshape(db, ds, d), lw["xattn_norm"], lw["xattn_wq"], cache_mem, lw["xattn_wo"],
                    layer=l, grp=_tile(db, 8), tq=ds).reshape(ts, d)
        last = l == depth - 1
        xp = _ffn(xp, lw["ffn2_norm"], lw["ffn2_wi"], lw["ffn2_wo"], fg, final_norm=last, tm=tm_p, tf=tf)
        xs = _ffn(xs, lw["ffn2_norm"], lw["ffn2_wi"], lw["ffn2_wo"], fg, final_norm=last, tm=tm_s, tf=tf)
        wlen = min(WINDOW, seq)
        outs["mla_p"].append(pr["lat"].reshape(batch, seq, A_LAT))
        outs["mla_s"].append(sp["lat"].reshape(db, ds, A_LAT))
        outs["diff_p"].append(pr["dkv"].reshape(batch, seq, 4 * B_D))
        outs["diff_s"].append(sp["dkv"].reshape(db, ds, 4 * B_D))
        outs["cmp_p"].append(pr["cmp"].reshape(batch, seq, LANE))
        outs["cmp_s"].append(sp["cmp"].reshape(db, ds, LANE))
        outs["sel_p"].append(pr["sel"].reshape(batch, seq, LANE))
        outs["sel_s"].append(sp["sel"].reshape(db, ds, LANE))
        outs["win_p"].append(pr["win"].reshape(batch, seq, LANE)[:, seq - wlen:])
        outs["win_s"].append(new_win)
        outs["mem_p"].append(mem_kv)
    st = {k: jnp.stack(v) for k, v in outs.items()}
    return (xp.reshape(batch, seq, d), xs.reshape(db, ds, d),
            st["mla_p"], st["mla_s"], st["diff_p"], st["diff_s"], st["cmp_p"], st["cmp_s"],
            st["sel_p"], st["sel_s"], st["win_p"], st["win_s"], st["mem_p"])
```

```python
import functools
import math

import numpy as np
import jax
import jax.numpy as jnp
from jax import lax
from jax.experimental import pallas as pl
from jax.experimental.pallas import tpu as pltpu

F32 = jnp.float32
BF16 = jnp.bfloat16

A_HEADS, A_NOPE, A_ROPE, A_VDIM, A_QRANK, A_KVRANK = 16, 64, 32, 64, 384, 128
A_LAT = A_KVRANK + A_ROPE
ROPE_BASE = 10000.0
B_HEADS, B_D = 4, 64
C_HEADS, C_DK, C_DV = 8, 64, 64
CMP_BLOCK, N_SEL, WINDOW = 64, 16, 512
M_HEADS, M_DH = 4, 128
PAGE_SIZE = 128
EPS = 1e-6
IN_SIZES = (A_QRANK, A_KVRANK, A_ROPE, B_HEADS * 2 * B_D, 4 * B_D, C_HEADS * C_DK,
            C_DK + C_DV, C_DK + C_DV, C_DK + C_DV, 3 * C_HEADS)

LANE = 128
SUBLANE = 8
VMEM_LIMIT = 56 * 1024 * 1024

MASKED = -2e30
M_INIT = -1e30
LOG2E = math.log2(math.e)

S_QC = 0
S_CKV = S_QC + A_QRANK
S_QB = S_CKV + LANE
S_DKV = S_QB + 2 * B_HEADS * LANE
S_QN = S_DKV + 4 * B_D
S_CMP = S_QN + C_HEADS * LANE
S_SEL = S_CMP + LANE
S_WIN = S_SEL + LANE
S_KR = S_WIN + LANE
S_KRR = S_KR + LANE
S_GL = S_KRR + LANE
N_WALL = S_GL + LANE

MIX_A = A_HEADS * A_VDIM
MIX_B = B_HEADS * LANE
MIX_C = C_HEADS * LANE
MIX_W = MIX_A + MIX_B + MIX_C


def _cparams(sem):
    return pltpu.CompilerParams(dimension_semantics=sem, vmem_limit_bytes=VMEM_LIMIT)


def _rms(x, g):
    return x * lax.rsqrt(jnp.mean(x * x, axis=-1, keepdims=True) + EPS) * g


def _dot(a, b):
    return jnp.dot(a, b, preferred_element_type=F32)


def _dot_t(a, b):
    return lax.dot_general(a, b, (((1,), (1,)), ((), ())), preferred_element_type=F32)


def _rep(x, n):
    if n == LANE:
        return x
    if n < LANE:
        return x[:, :n]
    return jnp.tile(x, (1, n // LANE))


def _log2(n):
    assert n > 0 and n & (n - 1) == 0, n
    return n.bit_length() - 1


def _div(x, n):
    return lax.shift_right_logical(x, _log2(n))


def _mod(x, n):
    return x & (n - 1)


def _alibi_col(rows, rows_per_head, n_heads):
    h = _div(lax.broadcasted_iota(jnp.int32, (rows, 1), 0), rows_per_head)
    return LOG2E * jnp.exp2(-(8.0 / n_heads) * (h + 1).astype(F32))


def _softmax_step(s, pv, m_ref, l_ref, acc_ref):
    kt = s.shape[1]
    m_prev = m_ref[...]
    m_next = jnp.maximum(m_prev, jnp.max(s, axis=1, keepdims=True))
    p = jnp.exp2(s - _rep(m_next, kt))
    alpha = jnp.exp2(m_prev - m_next)
    l_ref[...] = alpha * l_ref[...] + jnp.sum(p, axis=1, keepdims=True)
    m_ref[...] = m_next
    acc_ref[...] = acc_ref[...] * _rep(alpha, acc_ref.shape[1]) + pv(p.astype(BF16))


def _softmax_init(m_ref, l_ref, acc_ref):
    m_ref[...] = jnp.full(m_ref.shape, M_INIT, F32)
    l_ref[...] = jnp.zeros(l_ref.shape, F32)
    acc_ref[...] = jnp.zeros(acc_ref.shape, F32)


def _softmax_out(l_ref, acc_ref):
    inv = 1.0 / jnp.maximum(l_ref[...], 1e-30)
    return acc_ref[...] * _rep(inv, acc_ref.shape[1])


def _topk_mask(imp, valid, k):
    n = imp.shape[1]
    lane = lax.broadcasted_iota(jnp.int32, imp.shape, 1).astype(F32)
    work = jnp.where(valid, imp, -1.0)
    sel = jnp.zeros(imp.shape, F32)
    for _ in range(min(k, n)):
        mx = jnp.max(work, axis=1, keepdims=True)
        idx = jnp.min(jnp.where(work == mx, lane, float(n)), axis=1, keepdims=True)
        hit = lane == idx
        sel = jnp.where(hit & (mx >= 0.0), 1.0, sel)
        work = jnp.where(hit, -2.0, work)
    return sel


def _ffn_kernel(x_ref, g_ref, wg_ref, wu_ref, wo_ref, fg_ref, o_ref, xn_ref, acc_ref, *, final_norm):
    j = pl.program_id(1)

    @pl.when(j == 0)
    def _():
        xn_ref[...] = _rms(x_ref[...], g_ref[...]).astype(BF16)
        acc_ref[...] = jnp.zeros(acc_ref.shape, F32)

    xn = xn_ref[...]
    gate = _dot(xn, wg_ref[...])
    up = _dot(xn, wu_ref[...])
    h = (gate * (1.0 / (1.0 + jnp.exp(-gate))) * up).astype(BF16)
    acc_ref[...] += _dot(h, wo_ref[...])

    @pl.when(j == pl.num_programs(1) - 1)
    def _():
        y = x_ref[...] + 0.5 * acc_ref[...]
        if final_norm:
            y = _rms(y, fg_ref[...])
        o_ref[...] = y


def _ffn(x, g, wi, wo, fg, *, final_norm, tm, tf):
    t, d = x.shape
    dff = wo.shape[0]
    nj = dff // tf
    return pl.pallas_call(
        functools.partial(_ffn_kernel, final_norm=final_norm),
        grid=(t // tm, nj),
        in_specs=[
            pl.BlockSpec((tm, d), lambda i, j: (i, 0)),
            pl.BlockSpec((1, d), lambda i, j: (0, 0)),
            pl.BlockSpec((d, tf), lambda i, j: (0, j)),
            pl.BlockSpec((d, tf), lambda i, j: (0, j + nj)),
            pl.BlockSpec((tf, d), lambda i, j: (j, 0)),
            pl.BlockSpec((1, d), lambda i, j: (0, 0)),
        ],
        out_specs=pl.BlockSpec((tm, d), lambda i, j: (i, 0)),
        out_shape=jax.ShapeDtypeStruct((t, d), F32),
        scratch_shapes=[pltpu.VMEM((tm, d), BF16), pltpu.VMEM((tm, d), F32)],
        compiler_params=_cparams(("parallel", "arbitrary")),
        name="ffn",
    )(x, g, wi, wi, wo, fg)


def _mm_kernel(*refs, norm, residual):
    refs = list(refs)
    a_ref = refs.pop(0)
    g_ref = refs.pop(0) if norm else None
    w_ref = refs.pop(0)
    r_ref = refs.pop(0) if residual else None
    o_ref = refs.pop(0)
    a = a_ref[...]
    if norm:
        a = _rms(a, g_ref[...])
    y = _dot(a.astype(BF16), w_ref[...])
    if residual:
        y = y + r_ref[...]
    o_ref[...] = y


def _matmul(a, w, *, g=None, res=None, tm, tn, name):
    t, k = a.shape
    n = w.shape[1]
    args, specs = [a], [pl.BlockSpec((tm, k), lambda i, j: (i, 0))]
    if g is not None:
        args.append(g)
        specs.append(pl.BlockSpec((1, k), lambda i, j: (0, 0)))
    args.append(w)
    specs.append(pl.BlockSpec((k, tn), lambda i, j: (0, j)))
    if res is not None:
        args.append(res)
        specs.append(pl.BlockSpec((tm, tn), lambda i, j: (i, j)))
    return pl.pallas_call(
        functools.partial(_mm_kernel, norm=g is not None, residual=res is not None),
        grid=(t // tm, n // tn),
        in_specs=specs,
        out_specs=pl.BlockSpec((tm, tn), lambda i, j: (i, j)),
        out_shape=jax.ShapeDtypeStruct((t, n), F32),
        compiler_params=_cparams(("parallel", "parallel")),
        name=name,
    )(*args)


def _proj_kernel(x_ref, g_ref, w_ref, qg_ref, wn_ref, wr_ref, wrr_ref, wuk_ref, kvg_ref, cos_ref,
                 sin_ref, wp_ref,
                 qmla_ref, qd_ref, qn_ref, lat_ref, latk_ref, dkv_ref, dkvb_ref, cmp_ref, sel_ref,
                 selb_ref, win_ref, winb_ref, gates_ref, pool_ref):
    tm = x_ref.shape[0]
    qdt = qmla_ref.dtype
    xn = _rms(x_ref[...], g_ref[...]).astype(BF16)

    def seg(a, n):
        return _dot(xn, w_ref[:, a:a + n])

    def lane_tile(v, k):
        return v[:, k * LANE:(k + 1) * LANE]

    cos_t = cos_ref[...]
    sin_t = sin_ref[...]
    head = seg(S_QC, A_QRANK + A_KVRANK)
    qcn = _rms(head[:, 0:A_QRANK], qg_ref[...]).astype(BF16)
    sc_a = LOG2E * (A_NOPE + A_ROPE) ** -0.5
    hg = 4
    for g0 in range(0, A_HEADS, hg):
        cols = slice(g0 * LANE, (g0 + hg) * LANE)
        q_nope = _dot(qcn, wn_ref[:, cols]).astype(BF16)
        r1 = _dot(qcn, wr_ref[:, cols])
        r2 = _dot(qcn, wrr_ref[:, cols])
        for k in range(hg):
            q_lat = _dot(lane_tile(q_nope, k), wuk_ref[g0 + k])
            q_rope = lane_tile(r1, k) * cos_t + lane_tile(r2, k) * sin_t
            qmla_ref[g0 + k, :, 0:LANE] = (q_lat * sc_a).astype(qdt)
            qmla_ref[g0 + k, :, LANE:2 * LANE] = (q_rope * sc_a).astype(qdt)
    tail = seg(S_CMP, N_WALL - S_CMP)
    c = _rms(head[:, A_QRANK:A_QRANK + A_KVRANK], kvg_ref[...])
    kr = lane_tile(tail, 3) * cos_t + lane_tile(tail, 4) * sin_t
    lat_ref[:, 0:A_KVRANK] = c
    lat_ref[:, A_KVRANK:A_LAT] = kr[:, 0:A_ROPE]
    latk_ref[:, 0:LANE] = c.astype(BF16)
    latk_ref[:, LANE:2 * LANE] = kr.astype(BF16)
    sc_b = LOG2E * B_D ** -0.5
    hb = seg(S_QB, 2 * B_HEADS * LANE)
    for t in range(2 * B_HEADS):
        qd_ref[t] = (lane_tile(hb, t) * sc_b).astype(qdt)
    dkv = seg(S_DKV, 4 * B_D)
    dkv_ref[...] = dkv
    dkvb_ref[...] = dkv.astype(BF16)
    sc_c = LOG2E * C_DK ** -0.5
    hn = seg(S_QN, C_HEADS * LANE)
    for h in range(C_HEADS):
        qn_ref[h] = (lane_tile(hn, h) * sc_c).astype(qdt)
    cmp_r = lane_tile(tail, 0)
    cmp_ref[...] = cmp_r
    pool_ref[0] = jnp.sum(cmp_r.reshape(tm // CMP_BLOCK, CMP_BLOCK, LANE) * wp_ref[...][None], axis=1)
    sel_r = lane_tile(tail, 1)
    sel_ref[...] = sel_r
    selb_ref[...] = sel_r.astype(BF16)
    win_r = lane_tile(tail, 2)
    win_ref[...] = win_r
    winb_ref[...] = win_r.astype(BF16)
    gates_ref[...] = 1.0 / (1.0 + jnp.exp(-lane_tile(tail, 5)))


def _project(x, lw, cos_t, sin_t, *, tm, table_blocks, qdt):
    t, d = x.shape
    nt = t // tm
    const = lambda i: (0, 0)
    row = lambda i: (i, 0)
    one = pl.Buffered(1)
    in_specs = [
        pl.BlockSpec((tm, d), row),
        pl.BlockSpec((1, d), const),
        pl.BlockSpec((d, N_WALL), const, pipeline_mode=one),
        pl.BlockSpec((1, A_QRANK), const),
        pl.BlockSpec((A_QRANK, A_HEADS * LANE), const, pipeline_mode=one),
        pl.BlockSpec((A_QRANK, A_HEADS * LANE), const, pipeline_mode=one),
        pl.BlockSpec((A_QRANK, A_HEADS * LANE), const, pipeline_mode=one),
        pl.BlockSpec((A_HEADS, LANE, LANE), lambda i: (0, 0, 0), pipeline_mode=one),
        pl.BlockSpec((1, A_KVRANK), const),
        pl.BlockSpec((tm, LANE), lambda i: (i % table_blocks, 0)),
        pl.BlockSpec((tm, LANE), lambda i: (i % table_blocks, 0)),
        pl.BlockSpec((CMP_BLOCK, LANE), const),
    ]
    hrow = lambda i: (0, i, 0)
    outs = [
        (jax.ShapeDtypeStruct((A_HEADS, t, 2 * LANE), qdt), pl.BlockSpec((A_HEADS, tm, 2 * LANE), hrow)),
        (jax.ShapeDtypeStruct((2 * B_HEADS, t, LANE), qdt), pl.BlockSpec((2 * B_HEADS, tm, LANE), hrow)),
        (jax.ShapeDtypeStruct((C_HEADS, t, LANE), qdt), pl.BlockSpec((C_HEADS, tm, LANE), hrow)),
        (jax.ShapeDtypeStruct((t, A_LAT), F32), pl.BlockSpec((tm, A_LAT), row)),
        (jax.ShapeDtypeStruct((t, 2 * LANE), BF16), pl.BlockSpec((tm, 2 * LANE), row)),
        (jax.ShapeDtypeStruct((t, 4 * B_D), F32), pl.BlockSpec((tm, 4 * B_D), row)),
        (jax.ShapeDtypeStruct((t, 4 * B_D), BF16), pl.BlockSpec((tm, 4 * B_D), row)),
        (jax.ShapeDtypeStruct((t, LANE), F32), pl.BlockSpec((tm, LANE), row)),
        (jax.ShapeDtypeStruct((t, LANE), F32), pl.BlockSpec((tm, LANE), row)),
        (jax.ShapeDtypeStruct((t, LANE), BF16), pl.BlockSpec((tm, LANE), row)),
        (jax.ShapeDtypeStruct((t, LANE), F32), pl.BlockSpec((tm, LANE), row)),
        (jax.ShapeDtypeStruct((t, LANE), BF16), pl.BlockSpec((tm, LANE), row)),
        (jax.ShapeDtypeStruct((t, LANE), F32), pl.BlockSpec((tm, LANE), row)),
        (jax.ShapeDtypeStruct((nt, tm // CMP_BLOCK, LANE), F32),
         pl.BlockSpec((1, tm // CMP_BLOCK, LANE), lambda i: (i, 0, 0))),
    ]
    res = pl.pallas_call(
        _proj_kernel,
        grid=(nt,),
        in_specs=in_specs,
        out_specs=[o[1] for o in outs],
        out_shape=[o[0] for o in outs],
        compiler_params=_cparams(("parallel",)),
        name="project",
    )(x, lw["mix_norm"], lw["w_all"], lw["q_norm"], lw["wn"], lw["wr"], lw["wrr"], lw["wuk"],
      lw["kv_norm"], cos_t, sin_t, lw["w_pool"])
    keys = ("qmla", "qd", "qn", "lat", "latk", "dkv", "dkvb", "cmp", "sel", "selb", "win", "winb",
            "gates", "pool")
    out = dict(zip(keys, res))
    out["pool"] = out["pool"].reshape(t // CMP_BLOCK, LANE)
    return out


def _prompt_attn_kernel(qmla_ref, qd_ref, qn_ref, latk_ref, dkvb_ref, selb_ref, winb_ref, pool_ref,
                        gates_ref, wuv_ref, dng_ref, lam_ref, exp_ref, fb_ref, fc_ref, posf_ref, o_ref,
                        ma_ref, la_ref, acca_ref, mb_ref, lb_ref, accb_ref, mc_ref, lc_ref, accc_ref,
                        *, kt, lam_init):
    i = pl.program_id(1)
    qb = qmla_ref.shape[1]
    seq = latk_ref.shape[0]
    nblk = pool_ref.shape[0]
    ra, rb, rc = A_HEADS * qb, 2 * B_HEADS * qb, C_HEADS * qb
    q0 = i * qb
    qa = qmla_ref[...].reshape(ra, 2 * LANE)
    qn = qn_ref[...].reshape(rc, LANE)
    qd_aug = jnp.concatenate([qd_ref[...].reshape(rb, LANE), fb_ref[...]], axis=1)
    qn_aug = jnp.concatenate([qn, fc_ref[...]], axis=1)

    def qpos_rows(rows, width):
        return q0 + _mod(lax.broadcasted_iota(jnp.int32, (rows, width), 0), qb)

    slope_c = _alibi_col(rc, qb, C_HEADS)

    kcv = pool_ref[...].astype(BF16)
    blk = lax.broadcasted_iota(jnp.int32, (rc, nblk), 1)
    qp = qpos_rows(rc, nblk)
    cvalid = blk < _div(qp, CMP_BLOCK)
    cdist = (qp - ((blk + 1) * CMP_BLOCK - 1)).astype(F32)
    sc = jnp.where(cvalid, _dot_t(qn, kcv) - slope_c * cdist, MASKED)
    mcmp = jnp.max(sc, axis=1, keepdims=True)
    pc = jnp.where(cvalid, jnp.exp2(sc - mcmp), 0.0)
    pc = pc / jnp.maximum(jnp.sum(pc, axis=1, keepdims=True), 1e-30)
    o_cmp = _dot(pc.astype(BF16), kcv)
    imp = jnp.sum(pc.reshape(C_HEADS, qb, nblk), axis=0)
    sel = _topk_mask(imp, cvalid[:qb], N_SEL).astype(BF16)

    wlen = WINDOW + qb
    w0 = pl.multiple_of(jnp.maximum(q0 - WINDOW, 0), qb)
    wrows = winb_ref[pl.ds(w0, wlen), :]
    wd = qpos_rows(rc, wlen) - (w0 + lax.broadcasted_iota(jnp.int32, (rc, wlen), 1))
    sw = jnp.where((wd >= 0) & (wd <= WINDOW), _dot_t(qn, wrows) - slope_c * wd.astype(F32), MASKED)
    pw = jnp.exp2(sw - jnp.max(sw, axis=1, keepdims=True))
    o_win = _dot(pw.astype(BF16), wrows) / jnp.sum(pw, axis=1, keepdims=True)

    _softmax_init(ma_ref, la_ref, acca_ref)
    _softmax_init(mb_ref, lb_ref, accb_ref)
    _softmax_init(mc_ref, lc_ref, accc_ref)

    def tile(j, diagonal):
        k0 = pl.multiple_of(j * kt, kt)
        pf = posf_ref[pl.ds(k0, kt), :]
        lat = latk_ref[pl.ds(k0, kt), :]
        dk = dkvb_ref[pl.ds(k0, kt), :]
        ks = selb_ref[pl.ds(k0, kt), :]
        sa = _dot_t(qa, lat)
        sb = _dot_t(qd_aug, jnp.concatenate([dk[:, 0:LANE], pf], axis=1))
        ss = _dot_t(qn_aug, jnp.concatenate([ks, pf], axis=1))
        allowed = _dot(sel, exp_ref[j]) > 0.5
        if diagonal:
            sa = jnp.where(k0 + lax.broadcasted_iota(jnp.int32, (ra, kt), 1) <= qpos_rows(ra, kt), sa, MASKED)
            sb = jnp.where(k0 + lax.broadcasted_iota(jnp.int32, (rb, kt), 1) <= qpos_rows(rb, kt), sb, MASKED)
            qp1 = q0 + lax.broadcasted_iota(jnp.int32, (qb, kt), 0)
            kp1 = k0 + lax.broadcasted_iota(jnp.int32, (qb, kt), 1)
            allowed = allowed | ((_div(kp1, CMP_BLOCK) == _div(qp1, CMP_BLOCK)) & (kp1 <= qp1))
        ss = jnp.where(allowed[None], ss.reshape(C_HEADS, qb, kt), MASKED).reshape(rc, kt)
        _softmax_step(sa, lambda p: _dot(p, lat[:, 0:LANE]), ma_ref, la_ref, acca_ref)
        _softmax_step(sb, lambda p: _dot(p, dk[:, LANE:2 * LANE]), mb_ref, lb_ref, accb_ref)
        _softmax_step(ss, lambda p: _dot(p, ks), mc_ref, lc_ref, accc_ref)

    def full_tile(j, carry):
        tile(j, False)
        return carry

    n_full = q0 // kt
    lax.fori_loop(0, n_full, full_tile, 0)
    tile(n_full, True)

    o_lat = _softmax_out(la_ref, acca_ref).astype(BF16).reshape(A_HEADS, qb, LANE)
    for p in range(A_HEADS // 2):
        pair = jnp.concatenate([o_lat[2 * p], o_lat[2 * p + 1]], axis=-1)
        o_ref[:, p * LANE:(p + 1) * LANE] = _dot(pair, wuv_ref[p]).astype(o_ref.dtype)
    o_dif = _softmax_out(lb_ref, accb_ref).reshape(2 * B_HEADS, qb, LANE)
    lam = lam_ref[...]
    for h in range(B_HEADS):
        od = o_dif[2 * h] - lam * o_dif[2 * h + 1]
        od = _rms(od, dng_ref[...]) * (1.0 - lam_init)
        o_ref[:, MIX_A + h * LANE:MIX_A + (h + 1) * LANE] = od.astype(o_ref.dtype)
    o_sel = _softmax_out(lc_ref, accc_ref).reshape(C_HEADS, qb, LANE)
    o_cmp = o_cmp.reshape(C_HEADS, qb, LANE)
    o_win = o_win.reshape(C_HEADS, qb, LANE)
    gates = gates_ref[...]
    for h in range(C_HEADS):
        on = (gates[:, 3 * h:3 * h + 1] * o_cmp[h] + gates[:, 3 * h + 1:3 * h + 2] * o_sel[h]
              + gates[:, 3 * h + 2:3 * h + 3] * o_win[h])
        o_ref[:, MIX_A + MIX_B + h * LANE:MIX_A + MIX_B + (h + 1) * LANE] = on.astype(o_ref.dtype)


def _prompt_attn(pr, lw, lam_row, expand, feats, *, batch, seq, lam_init, qb=128, kt=512):
    kt = min(kt, seq)
    feat_b, feat_c, pos_f = feats
    nq = seq // qb
    nblk = seq // CMP_BLOCK
    hq = lambda b, i: (0, b * nq + i, 0)
    perb = lambda b, i: (b, 0)
    const2 = lambda b, i: (0, 0)
    const3 = lambda b, i: (0, 0, 0)
    ra, rb, rc = A_HEADS * qb, 2 * B_HEADS * qb, C_HEADS * qb
    stat = lambda r: [pltpu.VMEM((r, LANE), F32)] * 3
    return pl.pallas_call(
        functools.partial(_prompt_attn_kernel, kt=kt, lam_init=lam_init),
        grid=(batch, nq),
        in_specs=[
            pl.BlockSpec((A_HEADS, qb, 2 * LANE), hq),
            pl.BlockSpec((2 * B_HEADS, qb, LANE), hq),
            pl.BlockSpec((C_HEADS, qb, LANE), hq),
            pl.BlockSpec((seq, 2 * LANE), perb),
            pl.BlockSpec((seq, 4 * B_D), perb),
            pl.BlockSpec((seq, LANE), perb),
            pl.BlockSpec((seq, LANE), perb),
            pl.BlockSpec((nblk, LANE), perb),
            pl.BlockSpec((qb, LANE), lambda b, i: (b * nq + i, 0)),
            pl.BlockSpec((A_HEADS // 2, 2 * LANE, LANE), const3),
            pl.BlockSpec((1, LANE), const2),
            pl.BlockSpec((1, LANE), const2),
            pl.BlockSpec((seq // kt, nblk, kt), const3),
            pl.BlockSpec((rb, LANE), const2),
            pl.BlockSpec((rc, LANE), const2),
            pl.BlockSpec((seq, LANE), const2),
        ],
        out_specs=pl.BlockSpec((qb, MIX_W), lambda b, i: (b * nq + i, 0)),
        out_shape=jax.ShapeDtypeStruct((batch * seq, MIX_W), BF16),
        scratch_shapes=stat(ra) + stat(rb) + stat(rc),
        compiler_params=_cparams(("parallel", "arbitrary")),
        name="prompt_attn",
    )(pr["qmla"], pr["qd"], pr["qn"], pr["latk"], pr["dkvb"], pr["selb"], pr["winb"], pr["pool"],
      pr["gates"], lw["wuv_pair"], lw["diff_norm"], lam_row, expand, feat_b, feat_c, pos_f)


def _stream_pages(cache_ref, layer, pt_ref, ptn_ref, sem_ref, n_chunks, ppc, dst, consume):
    assert n_chunks % 2 == 0
    b = pl.program_id(0)
    nb = pl.num_programs(0)

    def page_copy(page, slot, p):
        return pltpu.make_async_copy(cache_ref.at[layer, page], dst(slot, p), sem_ref.at[slot])

    def start(table_ref, c, slot):
        for p in range(ppc):
            page_copy(table_ref[0, 0, c * ppc + p], slot, p).start()

    def wait(slot):
        for p in range(ppc):
            page_copy(0, slot, p).wait()

    @pl.when(b == 0)
    def _():
        start(pt_ref, 0, 0)

    def body(c, carry):
        slot = c % 2

        @pl.when(c + 1 < n_chunks)
        def _():
            start(pt_ref, c + 1, 1 - slot)

        @pl.when((c + 1 == n_chunks) & (b + 1 < nb))
        def _():
            start(ptn_ref, 0, 0)

        wait(slot)
        consume(c, slot)
        return carry

    lax.fori_loop(0, n_chunks, body, 0)


def _new_rows_tile(pad_ref, new_ref):
    pad_ref[...] = jnp.zeros(pad_ref.shape, F32)
    pad_ref[0:new_ref.shape[0], :] = new_ref[...]
    return pad_ref[...].astype(BF16)


def _dec_mla_kernel(pt_ref, ptn_ref, q_ref, new_ref, wuv_ref, cache_ref, o_ref,
                    buf_ref, sem_ref, pad_ref, m_ref, l_ref, acc_ref, *, layer, ppc):
    t = q_ref.shape[1]
    rows = A_HEADS * t
    n_pages = pt_ref.shape[2]
    q = q_ref[...].reshape(rows, 2 * LANE)[:, 0:A_LAT].astype(BF16)
    _softmax_init(m_ref, l_ref, acc_ref)

    def dst(slot, p):
        return buf_ref.at[slot, :, pl.ds(p * PAGE_SIZE, PAGE_SIZE)]

    def consume(c, slot):
        k_t = buf_ref[slot].astype(BF16)
        _softmax_step(_dot(q, k_t), lambda p: _dot_t(p, k_t[0:A_KVRANK, :]), m_ref, l_ref, acc_ref)

    _stream_pages(cache_ref, layer, pt_ref, ptn_ref, sem_ref, n_pages // ppc, ppc, dst, consume)
    kn = _new_rows_tile(pad_ref, new_ref)
    qj = _mod(lax.broadcasted_iota(jnp.int32, (rows, PAGE_SIZE), 0), t)
    kj = lax.broadcasted_iota(jnp.int32, (rows, PAGE_SIZE), 1)
    sn = jnp.where(kj <= qj, _dot_t(q, kn), MASKED)
    _softmax_step(sn, lambda p: _dot(p, kn[:, 0:A_KVRANK]), m_ref, l_ref, acc_ref)
    o_lat = _softmax_out(l_ref, acc_ref).astype(BF16)
    full = _dot(o_lat, wuv_ref[...]).reshape(A_HEADS, t, A_HEADS * A_VDIM)
    own = (_div(lax.broadcasted_iota(jnp.int32, (A_HEADS, 1, A_HEADS * A_VDIM), 2), A_VDIM)
           == lax.broadcasted_iota(jnp.int32, (A_HEADS, 1, A_HEADS * A_VDIM), 0))
    o_ref[...] = jnp.sum(jnp.where(own, full, 0.0), axis=0)


def _dec_diff_kernel(pt_ref, ptn_ref, q_ref, new_ref, dng_ref, lam_ref, cache_ref, o_ref,
                     buf_ref, sem_ref, pad_ref, m_ref, l_ref, acc_ref, *, layer, ppc, past, lam_init):
    t = q_ref.shape[1]
    rows = 2 * B_HEADS * t
    n_pages = pt_ref.shape[2]
    q = q_ref[...].reshape(rows, LANE).astype(BF16)
    slope = _alibi_col(rows, 2 * t, B_HEADS)
    _softmax_init(m_ref, l_ref, acc_ref)
    ck = ppc * PAGE_SIZE
    qpos = past + _mod(lax.broadcasted_iota(jnp.int32, (rows, ck), 0), t)

    def consume(c, slot):
        k = buf_ref[slot].reshape(ck, 4 * B_D).astype(BF16)
        dist = (qpos - (c * ck + lax.broadcasted_iota(jnp.int32, (rows, ck), 1))).astype(F32)
        _softmax_step(_dot_t(q, k[:, 0:LANE]) - slope * dist, lambda p: _dot(p, k[:, LANE:2 * LANE]),
                      m_ref, l_ref, acc_ref)

    _stream_pages(cache_ref, layer, pt_ref, ptn_ref, sem_ref, n_pages // ppc, ppc,
                  lambda slot, p: buf_ref.at[slot, p], consume)
    kn = _new_rows_tile(pad_ref, new_ref)
    dj = (_mod(lax.broadcasted_iota(jnp.int32, (rows, PAGE_SIZE), 0), t)
          - lax.broadcasted_iota(jnp.int32, (rows, PAGE_SIZE), 1))
    sn = jnp.where(dj >= 0, _dot_t(q, kn[:, 0:LANE]) - slope * dj.astype(F32), MASKED)
    _softmax_step(sn, lambda p: _dot(p, kn[:, LANE:2 * LANE]), m_ref, l_ref, acc_ref)
    o = _softmax_out(l_ref, acc_ref)
    lam = lam_ref[...]
    for h in range(B_HEADS):
        od = o[2 * h * t:(2 * h + 1) * t] - lam * o[(2 * h + 1) * t:(2 * h + 2) * t]
        o_ref[:, h * LANE:(h + 1) * LANE] = _rms(od, dng_ref[...]) * (1.0 - lam_init)


def _dec_nsa_kernel(pt_ref, ptn_ref, q_ref, seln_ref, winn_ref, wst_ref, gates_ref, wp_ref, exp_ref,
                    cmp_cache_ref, sel_cache_ref, o_ref, nwin_ref,
                    bufc_ref, semc_ref, bufs_ref, sems_ref, pad_ref, pool_ref, selrows_ref, wbuf_ref,
                    m_ref, l_ref, acc_ref, *, layer, ppc, past):
    t = q_ref.shape[1]
    rows = C_HEADS * t
    n_pages = pt_ref.shape[2]
    n_chunks = n_pages // ppc
    ck = ppc * PAGE_SIZE
    bpc = ck // CMP_BLOCK
    nblk = n_chunks * bpc
    q = q_ref[...].reshape(rows, LANE).astype(BF16)
    slope = _alibi_col(rows, t, C_HEADS)

    def pool(c, slot):
        r = bufc_ref[slot].reshape(bpc, CMP_BLOCK, LANE) * wp_ref[...][None]
        pool_ref[pl.ds(pl.multiple_of(c * bpc, bpc), bpc), :] = jnp.sum(r, axis=1)

    _stream_pages(cmp_cache_ref, layer, pt_ref, ptn_ref, semc_ref, n_chunks, ppc,
                  lambda slot, p: bufc_ref.at[slot, p], pool)
    kcv = pool_ref[...].astype(BF16)
    blk = lax.broadcasted_iota(jnp.int32, (rows, nblk), 1)
    qp = past + _mod(lax.broadcasted_iota(jnp.int32, (rows, nblk), 0), t)
    cvalid = blk < _div(qp, CMP_BLOCK)
    cdist = (qp - ((blk + 1) * CMP_BLOCK - 1)).astype(F32)
    sc = jnp.where(cvalid, _dot_t(q, kcv) - slope * cdist, MASKED)
    pc = jnp.where(cvalid, jnp.exp2(sc - jnp.max(sc, axis=1, keepdims=True)), 0.0)
    pc = pc / jnp.maximum(jnp.sum(pc, axis=1, keepdims=True), 1e-30)
    o_cmp = _dot(pc.astype(BF16), kcv)
    imp = jnp.sum(pc.reshape(C_HEADS, t, nblk), axis=0)
    sel = _topk_mask(imp, cvalid[:t], N_SEL)
    sel_rows = jnp.tile(sel, (C_HEADS, 1))
    for c in range(n_chunks):
        selrows_ref[c] = sel_rows[:, c * bpc:(c + 1) * bpc]

    _softmax_init(m_ref, l_ref, acc_ref)
    qpos = past + _mod(lax.broadcasted_iota(jnp.int32, (rows, ck), 0), t)

    def sweep(c, slot):
        k = bufs_ref[slot].reshape(ck, LANE).astype(BF16)
        chosen = _dot(selrows_ref[c].astype(BF16), exp_ref[...]) > 0.5
        dist = (qpos - (c * ck + lax.broadcasted_iota(jnp.int32, (rows, ck), 1))).astype(F32)
        s = jnp.where(chosen, _dot_t(q, k) - slope * dist, MASKED)
        _softmax_step(s, lambda p: _dot(p, k), m_ref, l_ref, acc_ref)

    _stream_pages(sel_cache_ref, layer, pt_ref, ptn_ref, sems_ref, n_chunks, ppc,
                  lambda slot, p: bufs_ref.at[slot, p], sweep)
    kn = _new_rows_tile(pad_ref, seln_ref)
    dj = (_mod(lax.broadcasted_iota(jnp.int32, (rows, PAGE_SIZE), 0), t)
          - lax.broadcasted_iota(jnp.int32, (rows, PAGE_SIZE), 1))
    sn = jnp.where(dj >= 0, _dot_t(q, kn) - slope * dj.astype(F32), MASKED)
    _softmax_step(sn, lambda p: _dot(p, kn), m_ref, l_ref, acc_ref)
    o_sel = _softmax_out(l_ref, acc_ref)

    wb = wst_ref.shape[0]
    wbuf_ref[...] = jnp.zeros(wbuf_ref.shape, F32)
    wbuf_ref[0:wb, :] = wst_ref[...]
    wbuf_ref[wb:wb + t, :] = winn_ref[...]
    wrows = wbuf_ref[...].astype(BF16)
    wl = wbuf_ref.shape[0]
    idx = lax.broadcasted_iota(jnp.int32, (rows, wl), 1)
    wpos = jnp.where(idx < wb, past - wb + idx, past + idx - wb)
    wd = past + _mod(lax.broadcasted_iota(jnp.int32, (rows, wl), 0), t) - wpos
    ok = (wd >= 0) & (wd <= WINDOW) & (idx < wb + t) & (wpos >= 0)
    sw = jnp.where(ok, _dot_t(q, wrows) - slope * wd.astype(F32), MASKED)
    pw = jnp.exp2(sw - jnp.max(sw, axis=1, keepdims=True))
    o_win = _dot(pw.astype(BF16), wrows) / jnp.sum(pw, axis=1, keepdims=True)
    nwin_ref[0:wb - t, :] = wst_ref[t:wb, :]
    nwin_ref[wb - t:wb, :] = winn_ref[...]

    gates = gates_ref[...]
    for h in range(C_HEADS):
        r = slice(h * t, (h + 1) * t)
        o_ref[:, h * LANE:(h + 1) * LANE] = (gates[:, 3 * h:3 * h + 1] * o_cmp[r]
                                              + gates[:, 3 * h + 1:3 * h + 2] * o_sel[r]
                                              + gates[:, 3 * h + 2:3 * h + 3] * o_win[r])


def _sample_attn(sp, lw, lam_row, expand_s, caches, win_state, page_table3, *, layer, lam_init, past,
                 ppc):
    cache_mla, cache_diff, cache_cmp, cache_sel = caches
    db, _, n_pages = page_table3.shape
    ts = sp["lat"].shape[0]
    t = ts // db
    wb = win_state.shape[2]
    pt_spec = pl.BlockSpec((1, 1, n_pages), lambda b: (b, 0, 0), memory_space=pltpu.SMEM)
    ptn_spec = pl.BlockSpec((1, 1, n_pages), lambda b: (jnp.minimum(b + 1, db - 1), 0, 0),
                            memory_space=pltpu.SMEM)
    hq = lambda b: (0, b, 0)
    row = lambda b: (b, 0)
    const2 = lambda b: (0, 0)
    any_spec = pl.BlockSpec(memory_space=pl.ANY)
    stat = lambda r: [pltpu.VMEM((r, LANE), F32)] * 3
    ck = ppc * PAGE_SIZE
    dma2 = pltpu.SemaphoreType.DMA((2,))

    def stream_scratch(width):
        return [pltpu.VMEM((2, ppc, PAGE_SIZE, width), F32), dma2]

    o_a = pl.pallas_call(
        functools.partial(_dec_mla_kernel, layer=layer, ppc=ppc),
        grid=(db,),
        in_specs=[pt_spec, ptn_spec,
                  pl.BlockSpec((A_HEADS, t, 2 * LANE), hq),
                  pl.BlockSpec((t, A_LAT), row),
                  pl.BlockSpec((A_KVRANK, A_HEADS * A_VDIM), const2),
                  any_spec],
        out_specs=pl.BlockSpec((t, MIX_A), row),
        out_shape=jax.ShapeDtypeStruct((ts, MIX_A), F32),
        scratch_shapes=([pltpu.VMEM((2, A_LAT, ck), F32), dma2, pltpu.VMEM((PAGE_SIZE, A_LAT), F32)]
                        + stat(A_HEADS * t)),
        compiler_params=_cparams(("arbitrary",)),
        name="sample_mla",
    )(page_table3, page_table3, sp["qmla"], sp["lat"], lw["wuv_flat"], cache_mla)

    o_b = pl.pallas_call(
        functools.partial(_dec_diff_kernel, layer=layer, ppc=ppc, past=past, lam_init=lam_init),
        grid=(db,),
        in_specs=[pt_spec, ptn_spec,
                  pl.BlockSpec((2 * B_HEADS, t, LANE), hq),
                  pl.BlockSpec((t, 4 * B_D), row),
                  pl.BlockSpec((1, LANE), const2),
                  pl.BlockSpec((1, LANE), const2),
                  any_spec],
        out_specs=pl.BlockSpec((t, MIX_B), row),
        out_shape=jax.ShapeDtypeStruct((ts, MIX_B), F32),
        scratch_shapes=(stream_scratch(4 * B_D) + [pltpu.VMEM((PAGE_SIZE, 4 * B_D), F32)]
                        + stat(2 * B_HEADS * t)),
        compiler_params=_cparams(("arbitrary",)),
        name="sample_diff",
    )(page_table3, page_table3, sp["qd"], sp["dkv"], lw["diff_norm"], lam_row, cache_diff)

    n_chunks = n_pages // ppc
    bpc = ppc * PAGE_SIZE // CMP_BLOCK
    rows_c = C_HEADS * t
    o_c, new_win = pl.pallas_call(
        functools.partial(_dec_nsa_kernel, layer=layer, ppc=ppc, past=past),
        grid=(db,),
        in_specs=[pt_spec, ptn_spec,
                  pl.BlockSpec((C_HEADS, t, LANE), hq),
                  pl.BlockSpec((t, LANE), row),
                  pl.BlockSpec((t, LANE), row),
                  pl.BlockSpec((None, None, wb, LANE), lambda b: (layer, b, 0, 0)),
                  pl.BlockSpec((t, LANE), row),
                  pl.BlockSpec((CMP_BLOCK, LANE), const2),
                  pl.BlockSpec((bpc, ppc * PAGE_SIZE), const2),
                  any_spec, any_spec],
        out_specs=[pl.BlockSpec((t, MIX_C), row),
                   pl.BlockSpec((None, wb, LANE), lambda b: (b, 0, 0))],
        out_shape=[jax.ShapeDtypeStruct((ts, MIX_C), F32),
                   jax.ShapeDtypeStruct((db, wb, LANE), F32)],
        scratch_shapes=(stream_scratch(LANE) + stream_scratch(LANE)
                        + [pltpu.VMEM((PAGE_SIZE, LANE), F32),
                           pltpu.VMEM((n_chunks * bpc, LANE), F32),
                           pltpu.VMEM((n_chunks, rows_c, bpc), F32),
                           pltpu.VMEM((wb + PAGE_SIZE, LANE), F32)]
                        + stat(rows_c)),
        compiler_params=_cparams(("arbitrary",)),
        name="sample_nsa",
    )(page_table3, page_table3, sp["qn"], sp["sel"], sp["win"], win_state, sp["gates"], lw["w_pool"],
      expand_s, cache_cmp, cache_sel)
    mixed = jnp.concatenate([o_a, o_b, o_c], axis=-1).astype(BF16)
    return mixed, new_win


def _xattn_kernel(x_ref, g_ref, wq_ref, mem_ref, wo_ref, o_ref, *, small):
    grp, tq, d = x_ref.shape
    x = x_ref[...].reshape(grp * tq, d)
    q = _dot(_rms(x, g_ref[...]).astype(BF16), wq_ref[...]) * (M_DH ** -0.5)
    hd = M_HEADS * M_DH
    outs = []
    for gi in range(grp):
        kv = mem_ref[gi]
        qg = q[gi * tq:(gi + 1) * tq]
        if not small:
            kv = kv.astype(BF16)
            qg = qg.astype(BF16)
        heads = []
        for h in range(M_HEADS):
            s = _dot_t(qg[:, h * M_DH:(h + 1) * M_DH], kv[:, h * M_DH:(h + 1) * M_DH])
            p = jnp.exp(s - jnp.max(s, axis=1, keepdims=True))
            inv = 1.0 / jnp.sum(p, axis=1, keepdims=True)
            if not small:
                p = p.astype(BF16)
            heads.append(_dot(p, kv[:, hd + h * M_DH:hd + (h + 1) * M_DH]) * inv)
        outs.append(jnp.concatenate(heads, axis=-1))
    attn = outs[0] if grp == 1 else jnp.concatenate(outs, axis=0)
    o_ref[...] = (x + _dot(attn.astype(BF16), wo_ref[...])).reshape(grp, tq, d)


def _xattn(x3, g, wq, mem, wo, *, layer, grp, tq):
    nb, t, d = x3.shape
    nm = mem.shape[-2]
    hd2 = mem.shape[-1]
    if mem.ndim == 4:
        mem_spec = pl.BlockSpec((None, grp, nm, hd2), lambda b, i: (layer, b, 0, 0))
    else:
        mem_spec = pl.BlockSpec((grp, nm, hd2), lambda b, i: (b, 0, 0))
    return pl.pallas_call(
        functools.partial(_xattn_kernel, small=tq < 16),
        grid=(nb // grp, t // tq),
        in_specs=[pl.BlockSpec((grp, tq, d), lambda b, i: (b, i, 0)),
                  pl.BlockSpec((1, d), lambda b, i: (0, 0)),
                  pl.BlockSpec((d, M_HEADS * M_DH), lambda b, i: (0, 0)),
                  mem_spec,
                  pl.BlockSpec((M_HEADS * M_DH, d), lambda b, i: (0, 0))],
        out_specs=pl.BlockSpec((grp, tq, d), lambda b, i: (b, i, 0)),
        out_shape=jax.ShapeDtypeStruct((nb, t, d), F32),
        compiler_params=_cparams(("parallel", "parallel")),
        name="xattn",
    )(x3, g, wq, mem, wo)


def _wall_columns():
    offs = np.concatenate([[0], np.cumsum(IN_SIZES)])
    o_qc, o_ckv, o_kr, o_qb, o_dkv, o_qn, o_cmp, o_sel, o_win, o_gl = offs[:-1]
    idx = np.zeros(N_WALL, np.int32)
    sgn = np.zeros(N_WALL, np.float32)

    def put(dst, src, sign=1.0):
        src = np.asarray(src)
        idx[dst:dst + len(src)] = src
        sgn[dst:dst + len(src)] = sign

    put(S_QC, o_qc + np.arange(A_QRANK))
    put(S_CKV, o_ckv + np.arange(A_KVRANK))
    for h in range(B_HEADS):
        put(S_QB + (2 * h) * LANE, o_qb + h * 2 * B_D + np.arange(B_D))
        put(S_QB + (2 * h + 1) * LANE + B_D, o_qb + h * 2 * B_D + B_D + np.arange(B_D))
    put(S_DKV, o_dkv + np.arange(4 * B_D))
    for h in range(C_HEADS):
        put(S_QN + h * LANE, o_qn + h * C_DK + np.arange(C_DK))
    put(S_CMP, o_cmp + np.arange(LANE))
    put(S_SEL, o_sel + np.arange(LANE))
    put(S_WIN, o_win + np.arange(LANE))
    half = A_ROPE // 2
    put(S_KR, o_kr + np.arange(A_ROPE))
    put(S_KRR, o_kr + half + np.arange(half), -1.0)
    put(S_KRR + half, o_kr + np.arange(half))
    put(S_GL, o_gl + np.arange(3 * C_HEADS))
    return idx, sgn


def _uq_columns():
    width = A_HEADS * LANE
    per = A_NOPE + A_ROPE
    half = A_ROPE // 2
    idx = np.zeros((3, width), np.int32)
    sgn = np.zeros((3, width), np.float32)
    for h in range(A_HEADS):
        idx[0, h * LANE:h * LANE + A_NOPE] = h * per + np.arange(A_NOPE)
        sgn[0, h * LANE:h * LANE + A_NOPE] = 1.0
        idx[1, h * LANE:h * LANE + A_ROPE] = h * per + A_NOPE + np.arange(A_ROPE)
        sgn[1, h * LANE:h * LANE + A_ROPE] = 1.0
        idx[2, h * LANE:h * LANE + half] = h * per + A_NOPE + half + np.arange(half)
        sgn[2, h * LANE:h * LANE + half] = -1.0
        idx[2, h * LANE + half:h * LANE + A_ROPE] = h * per + A_NOPE + np.arange(half)
        sgn[2, h * LANE + half:h * LANE + A_ROPE] = 1.0
    return idx, sgn


def _wout_rows():
    idx = np.zeros(MIX_W, np.int32)
    sgn = np.zeros(MIX_W, np.float32)
    idx[:MIX_A + MIX_B] = np.arange(MIX_A + MIX_B)
    sgn[:MIX_A + MIX_B] = 1.0
    for h in range(C_HEADS):
        dst = MIX_A + MIX_B + h * LANE + C_DK
        idx[dst:dst + C_DV] = MIX_A + MIX_B + h * C_DV + np.arange(C_DV)
        sgn[dst:dst + C_DV] = 1.0
    return idx, sgn


def _gather_cols(w, idx, sgn):
    return (jnp.take(w, jnp.asarray(idx), axis=1) * jnp.asarray(sgn)[None, :]).astype(BF16)


def _layer_weights(l, p):
    d = p["w_in"].shape[1]
    lw = {}
    lw["mix_norm"] = p["mix_norm"][l].reshape(1, d)
    lw["w_all"] = _gather_cols(p["w_in"][l], *_wall_columns())
    lw["q_norm"] = p["mla_q_norm"][l].reshape(1, A_QRANK)
    uidx, usgn = _uq_columns()
    lw["wn"], lw["wr"], lw["wrr"] = (_gather_cols(p["mla_w_uq"][l], uidx[k], usgn[k]) for k in range(3))
    wuk = jnp.transpose(p["mla_w_uk"][l], (1, 2, 0))
    lw["wuk"] = jnp.pad(wuk, ((0, 0), (0, LANE - A_NOPE), (0, 0))).astype(BF16)
    lw["kv_norm"] = p["mla_kv_norm"][l].reshape(1, A_KVRANK)
    wuv = p["mla_w_uv"][l]
    lw["wuv_flat"] = wuv.reshape(A_KVRANK, A_HEADS * A_VDIM).astype(BF16)
    z = jnp.zeros((A_KVRANK, A_VDIM), F32)
    pairs = [jnp.concatenate([jnp.concatenate([wuv[:, 2 * j], z], axis=1),
                              jnp.concatenate([z, wuv[:, 2 * j + 1]], axis=1)], axis=0)
             for j in range(A_HEADS // 2)]
    lw["wuv_pair"] = jnp.stack(pairs).astype(BF16)
    lw["diff_norm"] = p["diff_norm"][l].reshape(1, 2 * B_D)
    lw["w_pool"] = jnp.broadcast_to(p["nsa_w_cmp"][l][:, None], (CMP_BLOCK, LANE))
    ridx, rsgn = _wout_rows()
    lw["w_out"] = (jnp.take(p["w_out"][l], jnp.asarray(ridx), axis=0) * jnp.asarray(rsgn)[:, None]).astype(BF16)
    for k in ("ffn1_norm", "ffn2_norm", "xattn_norm", "mem_norm"):
        lw[k] = p[k][l].reshape(1, d)
    for k in ("ffn1_wi", "ffn1_wo", "ffn2_wi", "ffn2_wo", "xattn_wq", "xattn_wkv", "xattn_wo"):
        lw[k] = p[k][l].astype(BF16)
    return lw


def _rope_tables(pos):
    half = A_ROPE // 2
    inv = ROPE_BASE ** (-jnp.arange(half, dtype=F32) / half)
    ang = pos.astype(F32)[:, None] * inv
    pad = jnp.zeros((pos.shape[0], LANE - A_ROPE), F32)
    cos_t = jnp.concatenate([jnp.cos(ang), jnp.cos(ang), pad], axis=1)
    sin_t = jnp.concatenate([jnp.sin(ang), jnp.sin(ang), pad], axis=1)
    return cos_t, sin_t


def _slope_feats(n_heads, tiles_per_head, rows_per_tile):
    c = LOG2E * jnp.exp2(-(8.0 / n_heads) * (jnp.arange(n_heads, dtype=F32) + 1.0))
    c1 = c.astype(BF16).astype(F32)
    c2 = (c - c1).astype(BF16).astype(F32)
    c3 = (c - c1 - c2).astype(BF16).astype(F32)
    row = jnp.stack([c1, c2, c3, c1, c2, c3], axis=1)
    row = jnp.pad(row, ((0, 0), (0, LANE - 6)))
    return jnp.repeat(row, tiles_per_head * rows_per_tile, axis=0).astype(BF16)


def _pos_feats(pos):
    pos = np.asarray(pos, np.int64)
    lo = pos % CMP_BLOCK
    hi = pos - lo
    assert np.all(np.abs(hi) // CMP_BLOCK <= 256)
    f = np.zeros((len(pos), LANE), np.float32)
    f[:, 0:3] = hi[:, None]
    f[:, 3:6] = lo[:, None]
    return jnp.asarray(f, BF16)


def _block_expander(n_blocks, n_keys):
    e = (np.arange(n_keys)[None, :] // CMP_BLOCK) == np.arange(n_blocks)[:, None]
    return e.astype(np.float32)


def _tile(n, pref):
    t = min(n, pref)
    assert n % t == 0, (n, t)
    return t


def kernel(x_prompt, x_sample, mem_prompt, cache_mla, cache_diff, cache_nsa_cmp, cache_nsa_sel, state_nsa_win, cache_mem, page_table, ffn1_norm, ffn1_wi, ffn1_wo, mix_norm, w_in, mla_q_norm, mla_w_uq, mla_kv_norm, mla_w_uk, mla_w_uv, diff_lam, diff_norm, nsa_w_cmp, w_out, xattn_norm, mem_norm, xattn_wq, xattn_wkv, xattn_wo, ffn2_norm, ffn2_wi, ffn2_wo, final_norm):
    params = dict(ffn1_norm=ffn1_norm, ffn1_wi=ffn1_wi, ffn1_wo=ffn1_wo, mix_norm=mix_norm, w_in=w_in,
                  mla_q_norm=mla_q_norm, mla_w_uq=mla_w_uq, mla_kv_norm=mla_kv_norm, mla_w_uk=mla_w_uk,
                  mla_w_uv=mla_w_uv, diff_norm=diff_norm, nsa_w_cmp=nsa_w_cmp, w_out=w_out,
                  xattn_norm=xattn_norm, mem_norm=mem_norm, xattn_wq=xattn_wq, xattn_wkv=xattn_wkv,
                  xattn_wo=xattn_wo, ffn2_norm=ffn2_norm, ffn2_wi=ffn2_wi, ffn2_wo=ffn2_wo)
    batch, seq, d = x_prompt.shape
    db, ds, _ = x_sample.shape
    depth = w_in.shape[0]
    n_mem = mem_prompt.shape[1]
    n_pages = page_table.shape[1]
    past = n_pages * PAGE_SIZE
    tp, ts = batch * seq, db * ds
    assert ds % SUBLANE == 0 and seq % 128 == 0 and state_nsa_win.shape[2] == min(WINDOW, past)
    tm_p = _tile(seq, 512)
    tm_s = _tile(ts, 256)
    assert tm_s % ds == 0
    tf = _tile(ffn1_wo.shape[1], 512)
    tn = _tile(d, 1024)

    cos_p, sin_p = _rope_tables(jnp.arange(seq))
    cos_s, sin_s = _rope_tables(past + (jnp.arange(tm_s) % ds))
    kt = min(512, seq)
    expand_p = jnp.asarray(_block_expander(seq // CMP_BLOCK, seq).reshape(seq // CMP_BLOCK, seq // kt, kt)
                           .transpose(1, 0, 2), BF16)
    feats_p = (_slope_feats(B_HEADS, 2, 128), _slope_feats(C_HEADS, 1, 128), _pos_feats(np.arange(seq)))
    ppc = min(32, n_pages // 2)
    assert n_pages % (2 * ppc) == 0
    cache_mla_t = jnp.swapaxes(cache_mla, 2, 3)
    expand_s = jnp.asarray(_block_expander(ppc * PAGE_SIZE // CMP_BLOCK, ppc * PAGE_SIZE), BF16)
    page_table3 = page_table.reshape(db, 1, n_pages)
    fg = final_norm.reshape(1, d)

    xp = x_prompt.reshape(tp, d)
    xs = x_sample.reshape(ts, d)
    mem2 = mem_prompt.reshape(batch * n_mem, d)
    outs = {k: [] for k in ("mla_p", "mla_s", "diff_p", "diff_s", "cmp_p", "cmp_s", "sel_p", "sel_s",
                            "win_p", "win_s", "mem_p")}
    for l in range(depth):
        lw = _layer_weights(l, params)
        lam_init = 0.8 - 0.6 * math.exp(-0.3 * l)
        dl = diff_lam[l].astype(F32)
        lam = jnp.exp(jnp.sum(dl[0] * dl[1])) - jnp.exp(jnp.sum(dl[2] * dl[3])) + lam_init
        lam_row = jnp.full((1, LANE), lam, F32)
        xp = _ffn(xp, lw["ffn1_norm"], lw["ffn1_wi"], lw["ffn1_wo"], fg, final_norm=False, tm=tm_p, tf=tf)
        xs = _ffn(xs, lw["ffn1_norm"], lw["ffn1_wi"], lw["ffn1_wo"], fg, final_norm=False, tm=tm_s, tf=tf)
        pr = _project(xp, lw, cos_p, sin_p, tm=tm_p, table_blocks=seq // tm_p, qdt=BF16)
        mix_p = _prompt_attn(pr, lw, lam_row, expand_p, feats_p, batch=batch, seq=seq, lam_init=lam_init,
                             kt=kt)
        xp = _matmul(mix_p, lw["w_out"], res=xp, tm=tm_p, tn=tn, name="w_out")
        sp = _project(xs, lw, cos_s, sin_s, tm=tm_s, table_blocks=1, qdt=F32)
        mix_s, new_win = _sample_attn(sp, lw, lam_row, expand_s,
                                      (cache_mla_t, cache_diff, cache_nsa_cmp, cache_nsa_sel),
                                      state_nsa_win, page_table3, layer=l, lam_init=lam_init, past=past,
                                      ppc=ppc)
        xs = _matmul(mix_s, lw["w_out"], res=xs, tm=tm_s, tn=tn, name="w_out_s")
        mem_kv = _matmul(mem2, lw["xattn_wkv"], g=lw["mem_norm"], tm=_tile(batch * n_mem, 512),
                         tn=_tile(lw["xattn_wkv"].shape[1], 1024), name="mem_kv")
        mem_kv = mem_kv.reshape(batch, n_mem, -1)
        xp = _xattn(xp.reshape(batch, seq, d), lw["xattn_norm"], lw["xattn_wq"], mem_kv, lw["xattn_wo"],
                    layer=l, grp=1, tq=tm_p).reshape(tp, d)
        xs = _xattn(xs.reshape(db, ds, d), lw["xattn_norm"], lw["xattn_wq"], cache_mem, lw["xattn_wo"],
                    layer=l, grp=_tile(db, 8), tq=ds).reshape(ts, d)
        last = l == depth - 1
        xp = _ffn(xp, lw["ffn2_norm"], lw["ffn2_wi"], lw["ffn2_wo"], fg, final_norm=last, tm=tm_p, tf=tf)
        xs = _ffn(xs, lw["ffn2_norm"], lw["ffn2_wi"], lw["ffn2_wo"], fg, final_norm=last, tm=tm_s, tf=tf)
        wlen = min(WINDOW, seq)
        outs["mla_p"].append(pr["lat"].reshape(batch, seq, A_LAT))
        outs["mla_s"].append(sp["lat"].reshape(db, ds, A_LAT))
        outs["diff_p"].append(pr["dkv"].reshape(batch, seq, 4 * B_D))
        outs["diff_s"].append(sp["dkv"].reshape(db, ds, 4 * B_D))
        outs["cmp_p"].append(pr["cmp"].reshape(batch, seq, LANE))
        outs["cmp_s"].append(sp["cmp"].reshape(db, ds, LANE))
        outs["sel_p"].append(pr["sel"].reshape(batch, seq, LANE))
        outs["sel_s"].append(sp["sel"].reshape(db, ds, LANE))
        outs["win_p"].append(pr["win"].reshape(batch, seq, LANE)[:, seq - wlen:])
        outs["win_s"].append(new_win)
        outs["mem_p"].append(mem_kv)
    st = {k: jnp.stack(v) for k, v in outs.items()}
    return (xp.reshape(batch, seq, d), xs.reshape(db, ds, d),
            st["mla_p"], st["mla_s"], st["diff_p"], st["diff_s"], st["cmp_p"], st["cmp_s"],
            st["sel_p"], st["sel_s"], st["win_p"], st["win_s"], st["mem_p"])
```

```python
import functools
import math

import numpy as np
import jax
import jax.numpy as jnp
from jax import lax
from jax.experimental import pallas as pl
from jax.experimental.pallas import tpu as pltpu

F32 = jnp.float32
BF16 = jnp.bfloat16

A_HEADS, A_NOPE, A_ROPE, A_VDIM, A_QRANK, A_KVRANK = 16, 64, 32, 64, 384, 128
A_LAT = A_KVRANK + A_ROPE
ROPE_BASE = 10000.0
B_HEADS, B_D = 4, 64
C_HEADS, C_DK, C_DV = 8, 64, 64
CMP_BLOCK, N_SEL, WINDOW = 64, 16, 512
M_HEADS, M_DH = 4, 128
PAGE_SIZE = 128
EPS = 1e-6
IN_SIZES = (A_QRANK, A_KVRANK, A_ROPE, B_HEADS * 2 * B_D, 4 * B_D, C_HEADS * C_DK,
            C_DK + C_DV, C_DK + C_DV, C_DK + C_DV, 3 * C_HEADS)

LANE = 128
SUBLANE = 8
VMEM_LIMIT = 56 * 1024 * 1024

MASKED = -2e30
M_INIT = -1e30
LOG2E = math.log2(math.e)

S_QC = 0
S_CKV = S_QC + A_QRANK
S_QB = S_CKV + LANE
S_DKV = S_QB + 2 * B_HEADS * LANE
S_QN = S_DKV + 4 * B_D
S_CMP = S_QN + C_HEADS * LANE
S_SEL = S_CMP + LANE
S_WIN = S_SEL + LANE
S_KR = S_WIN + LANE
S_KRR = S_KR + LANE
S_GL = S_KRR + LANE
N_WALL = S_GL + LANE

MIX_A = A_HEADS * A_VDIM
MIX_B = B_HEADS * LANE
MIX_C = C_HEADS * LANE
MIX_W = MIX_A + MIX_B + MIX_C


def _cparams(sem):
    return pltpu.CompilerParams(dimension_semantics=sem, vmem_limit_bytes=VMEM_LIMIT)


def _rms(x, g):
    return x * lax.rsqrt(jnp.mean(x * x, axis=-1, keepdims=True) + EPS) * g


def _dot(a, b):
    return jnp.dot(a, b, preferred_element_type=F32)


def _dot_t(a, b):
    return lax.dot_general(a, b, (((1,), (1,)), ((), ())), preferred_element_type=F32)


def _rep(x, n):
    if n == LANE:
        return x
    if n < LANE:
        return x[:, :n]
    return jnp.tile(x, (1, n // LANE))


def _log2(n):
    assert n > 0 and n & (n - 1) == 0, n
    return n.bit_length() - 1


def _div(x, n):
    return lax.shift_right_logical(x, _log2(n))


def _mod(x, n):
    return x & (n - 1)


def _alibi_col(rows, rows_per_head, n_heads):
    h = _div(lax.broadcasted_iota(jnp.int32, (rows, 1), 0), rows_per_head)
    return LOG2E * jnp.exp2(-(8.0 / n_heads) * (h + 1).astype(F32))


def _softmax_step(s, pv, m_ref, l_ref, acc_ref):
    kt = s.shape[1]
    m_prev = m_ref[...]
    m_next = jnp.maximum(m_prev, jnp.max(s, axis=1, keepdims=True))
    p = jnp.exp2(s - _rep(m_next, kt))
    alpha = jnp.exp2(m_prev - m_next)
    l_ref[...] = alpha * l_ref[...] + jnp.sum(p, axis=1, keepdims=True)
    m_ref[...] = m_next
    acc_ref[...] = acc_ref[...] * _rep(alpha, acc_ref.shape[1]) + pv(p.astype(BF16))


def _softmax_init(m_ref, l_ref, acc_ref):
    m_ref[...] = jnp.full(m_ref.shape, M_INIT, F32)
    l_ref[...] = jnp.zeros(l_ref.shape, F32)
    acc_ref[...] = jnp.zeros(acc_ref.shape, F32)


def _softmax_out(l_ref, acc_ref):
    inv = 1.0 / jnp.maximum(l_ref[...], 1e-30)
    return acc_ref[...] * _rep(inv, acc_ref.shape[1])


def _topk_mask(imp, valid, k):
    n = imp.shape[1]
    lane = lax.broadcasted_iota(jnp.int32, imp.shape, 1).astype(F32)
    work = jnp.where(valid, imp, -1.0)
    sel = jnp.zeros(imp.shape, F32)
    for _ in range(min(k, n)):
        mx = jnp.max(work, axis=1, keepdims=True)
        idx = jnp.min(jnp.where(work == mx, lane, float(n)), axis=1, keepdims=True)
        hit = lane == idx
        sel = jnp.where(hit & (mx >= 0.0), 1.0, sel)
        work = jnp.where(hit, -2.0, work)
    return sel


def _topk_mask_by_rank(imp, valid, k):
    r, n = imp.shape
    x = jnp.where(valid, imp, -1.0)
    jj = lax.broadcasted_iota(jnp.int32, (n, n), 0)
    cc = lax.broadcasted_iota(jnp.int32, (n, n), 1)
    eye = jnp.where(jj == cc, 1.0, 0.0)
    lower = jnp.where(jj < cc, 1.0, 0.0)
    rows = []
    for i in range(r):
        row = x[i:i + 1, :]
        col = jnp.sum(eye * row, axis=1, keepdims=True)
        beats = jnp.where(col > row, 1.0, jnp.where(col == row, lower, 0.0))
        rank = jnp.sum(beats, axis=0, keepdims=True)
        rows.append(jnp.where((rank < float(k)) & (row >= 0.0), 1.0, 0.0))
    return jnp.concatenate(rows, axis=0)


def _ffn_kernel(x_ref, g_ref, wg_ref, wu_ref, wo_ref, fg_ref, o_ref, xn_ref, acc_ref, *, final_norm):
    j = pl.program_id(1)

    @pl.when(j == 0)
    def _():
        xn_ref[...] = _rms(x_ref[...], g_ref[...]).astype(BF16)
        acc_ref[...] = jnp.zeros(acc_ref.shape, F32)

    xn = xn_ref[...]
    gate = _dot(xn, wg_ref[...])
    up = _dot(xn, wu_ref[...])
    h = (gate * (1.0 / (1.0 + jnp.exp(-gate))) * up).astype(BF16)
    acc_ref[...] += _dot(h, wo_ref[...])

    @pl.when(j == pl.num_programs(1) - 1)
    def _():
        y = x_ref[...] + 0.5 * acc_ref[...]
        if final_norm:
            y = _rms(y, fg_ref[...])
        o_ref[...] = y


def _ffn(x, g, wi, wo, fg, *, final_norm, tm, tf):
    t, d = x.shape
    dff = wo.shape[0]
    nj = dff // tf
    return pl.pallas_call(
        functools.partial(_ffn_kernel, final_norm=final_norm),
        grid=(t // tm, nj),
        in_specs=[
            pl.BlockSpec((tm, d), lambda i, j: (i, 0)),
            pl.BlockSpec((1, d), lambda i, j: (0, 0)),
            pl.BlockSpec((d, tf), lambda i, j: (0, j)),
            pl.BlockSpec((d, tf), lambda i, j: (0, j + nj)),
            pl.BlockSpec((tf, d), lambda i, j: (j, 0)),
            pl.BlockSpec((1, d), lambda i, j: (0, 0)),
        ],
        out_specs=pl.BlockSpec((tm, d), lambda i, j: (i, 0)),
        out_shape=jax.ShapeDtypeStruct((t, d), F32),
        scratch_shapes=[pltpu.VMEM((tm, d), BF16), pltpu.VMEM((tm, d), F32)],
        compiler_params=_cparams(("parallel", "arbitrary")),
        name="ffn",
    )(x, g, wi, wi, wo, fg)


def _mm_kernel(*refs, norm, residual):
    refs = list(refs)
    a_ref = refs.pop(0)
    g_ref = refs.pop(0) if norm else None
    w_ref = refs.pop(0)
    r_ref = refs.pop(0) if residual else None
    o_ref = refs.pop(0)
    a = a_ref[...]
    if norm:
        a = _rms(a, g_ref[...])
    y = _dot(a.astype(BF16), w_ref[...])
    if residual:
        y = y + r_ref[...]
    o_ref[...] = y


def _matmul(a, w, *, g=None, res=None, tm, tn, name):
    t, k = a.shape
    n = w.shape[1]
    args, specs = [a], [pl.BlockSpec((tm, k), lambda i, j: (i, 0))]
    if g is not None:
        args.append(g)
        specs.append(pl.BlockSpec((1, k), lambda i, j: (0, 0)))
    args.append(w)
    specs.append(pl.BlockSpec((k, tn), lambda i, j: (0, j)))
    if res is not None:
        args.append(res)
        specs.append(pl.BlockSpec((tm, tn), lambda i, j: (i, j)))
    return pl.pallas_call(
        functools.partial(_mm_kernel, norm=g is not None, residual=res is not None),
        grid=(t // tm, n // tn),
        in_specs=specs,
        out_specs=pl.BlockSpec((tm, tn), lambda i, j: (i, j)),
        out_shape=jax.ShapeDtypeStruct((t, n), F32),
        compiler_params=_cparams(("parallel", "parallel")),
        name=name,
    )(*args)


def _proj_kernel(x_ref, g_ref, w_ref, qg_ref, wn_ref, wr_ref, wrr_ref, wuk_ref, kvg_ref, cos_ref,
                 sin_ref, wp_ref,
                 qmla_ref, qd_ref, qn_ref, lat_ref, latk_ref, dkv_ref, dkvb_ref, cmp_ref, sel_ref,
                 selb_ref, win_ref, winb_ref, gates_ref, pool_ref):
    tm = x_ref.shape[0]
    qdt = qmla_ref.dtype
    xn = _rms(x_ref[...], g_ref[...]).astype(BF16)

    def seg(a, n):
        return _dot(xn, w_ref[:, a:a + n])

    def lane_tile(v, k):
        return v[:, k * LANE:(k + 1) * LANE]

    cos_t = cos_ref[...]
    sin_t = sin_ref[...]
    head = seg(S_QC, A_QRANK + A_KVRANK)
    qcn = _rms(head[:, 0:A_QRANK], qg_ref[...]).astype(BF16)
    sc_a = LOG2E * (A_NOPE + A_ROPE) ** -0.5
    hg = 4
    for g0 in range(0, A_HEADS, hg):
        cols = slice(g0 * LANE, (g0 + hg) * LANE)
        q_nope = _dot(qcn, wn_ref[:, cols]).astype(BF16)
        r1 = _dot(qcn, wr_ref[:, cols])
        r2 = _dot(qcn, wrr_ref[:, cols])
        for k in range(hg):
            q_lat = _dot(lane_tile(q_nope, k), wuk_ref[g0 + k])
            q_rope = lane_tile(r1, k) * cos_t + lane_tile(r2, k) * sin_t
            qmla_ref[g0 + k, :, 0:LANE] = (q_lat * sc_a).astype(qdt)
            qmla_ref[g0 + k, :, LANE:2 * LANE] = (q_rope * sc_a).astype(qdt)
    tail = seg(S_CMP, N_WALL - S_CMP)
    c = _rms(head[:, A_QRANK:A_QRANK + A_KVRANK], kvg_ref[...])
    kr = lane_tile(tail, 3) * cos_t + lane_tile(tail, 4) * sin_t
    lat_ref[:, 0:A_KVRANK] = c
    lat_ref[:, A_KVRANK:A_LAT] = kr[:, 0:A_ROPE]
    latk_ref[:, 0:LANE] = c.astype(BF16)
    latk_ref[:, LANE:2 * LANE] = kr.astype(BF16)
    sc_b = LOG2E * B_D ** -0.5
    hb = seg(S_QB, 2 * B_HEADS * LANE)
    for t in range(2 * B_HEADS):
        qd_ref[t] = (lane_tile(hb, t) * sc_b).astype(qdt)
    dkv = seg(S_DKV, 4 * B_D)
    dkv_ref[...] = dkv
    dkvb_ref[...] = dkv.astype(BF16)
    sc_c = LOG2E * C_DK ** -0.5
    hn = seg(S_QN, C_HEADS * LANE)
    for h in range(C_HEADS):
        qn_ref[h] = (lane_tile(hn, h) * sc_c).astype(qdt)
    cmp_r = lane_tile(tail, 0)
    cmp_ref[...] = cmp_r
    pool_ref[0] = jnp.sum(cmp_r.reshape(tm // CMP_BLOCK, CMP_BLOCK, LANE) * wp_ref[...][None], axis=1)
    sel_r = lane_tile(tail, 1)
    sel_ref[...] = sel_r
    selb_ref[...] = sel_r.astype(BF16)
    win_r = lane_tile(tail, 2)
    win_ref[...] = win_r
    winb_ref[...] = win_r.astype(BF16)
    gates_ref[...] = 1.0 / (1.0 + jnp.exp(-lane_tile(tail, 5)))


def _project(x, lw, cos_t, sin_t, *, tm, table_blocks, qdt):
    t, d = x.shape
    nt = t // tm
    const = lambda i: (0, 0)
    row = lambda i: (i, 0)
    one = pl.Buffered(1)
    in_specs = [
        pl.BlockSpec((tm, d), row),
        pl.BlockSpec((1, d), const),
        pl.BlockSpec((d, N_WALL), const, pipeline_mode=one),
        pl.BlockSpec((1, A_QRANK), const),
        pl.BlockSpec((A_QRANK, A_HEADS * LANE), const, pipeline_mode=one),
        pl.BlockSpec((A_QRANK, A_HEADS * LANE), const, pipeline_mode=one),
        pl.BlockSpec((A_QRANK, A_HEADS * LANE), const, pipeline_mode=one),
        pl.BlockSpec((A_HEADS, LANE, LANE), lambda i: (0, 0, 0), pipeline_mode=one),
        pl.BlockSpec((1, A_KVRANK), const),
        pl.BlockSpec((tm, LANE), lambda i: (i % table_blocks, 0)),
        pl.BlockSpec((tm, LANE), lambda i: (i % table_blocks, 0)),
        pl.BlockSpec((CMP_BLOCK, LANE), const),
    ]
    hrow = lambda i: (0, i, 0)
    outs = [
        (jax.ShapeDtypeStruct((A_HEADS, t, 2 * LANE), qdt), pl.BlockSpec((A_HEADS, tm, 2 * LANE), hrow)),
        (jax.ShapeDtypeStruct((2 * B_HEADS, t, LANE), qdt), pl.BlockSpec((2 * B_HEADS, tm, LANE), hrow)),
        (jax.ShapeDtypeStruct((C_HEADS, t, LANE), qdt), pl.BlockSpec((C_HEADS, tm, LANE), hrow)),
        (jax.ShapeDtypeStruct((t, A_LAT), F32), pl.BlockSpec((tm, A_LAT), row)),
        (jax.ShapeDtypeStruct((t, 2 * LANE), BF16), pl.BlockSpec((tm, 2 * LANE), row)),
        (jax.ShapeDtypeStruct((t, 4 * B_D), F32), pl.BlockSpec((tm, 4 * B_D), row)),
        (jax.ShapeDtypeStruct((t, 4 * B_D), BF16), pl.BlockSpec((tm, 4 * B_D), row)),
        (jax.ShapeDtypeStruct((t, LANE), F32), pl.BlockSpec((tm, LANE), row)),
        (jax.ShapeDtypeStruct((t, LANE), F32), pl.BlockSpec((tm, LANE), row)),
        (jax.ShapeDtypeStruct((t, LANE), BF16), pl.BlockSpec((tm, LANE), row)),
        (jax.ShapeDtypeStruct((t, LANE), F32), pl.BlockSpec((tm, LANE), row)),
        (jax.ShapeDtypeStruct((t, LANE), BF16), pl.BlockSpec((tm, LANE), row)),
        (jax.ShapeDtypeStruct((t, LANE), F32), pl.BlockSpec((tm, LANE), row)),
        (jax.ShapeDtypeStruct((nt, tm // CMP_BLOCK, LANE), F32),
         pl.BlockSpec((1, tm // CMP_BLOCK, LANE), lambda i: (i, 0, 0))),
    ]
    res = pl.pallas_call(
        _proj_kernel,
        grid=(nt,),
        in_specs=in_specs,
        out_specs=[o[1] for o in outs],
        out_shape=[o[0] for o in outs],
        compiler_params=_cparams(("parallel",)),
        name="project",
    )(x, lw["mix_norm"], lw["w_all"], lw["q_norm"], lw["wn"], lw["wr"], lw["wrr"], lw["wuk"],
      lw["kv_norm"], cos_t, sin_t, lw["w_pool"])
    keys = ("qmla", "qd", "qn", "lat", "latk", "dkv", "dkvb", "cmp", "sel", "selb", "win", "winb",
            "gates", "pool")
    out = dict(zip(keys, res))
    out["pool"] = out["pool"].reshape(t // CMP_BLOCK, LANE)
    return out


def _prompt_attn_kernel(qmla_ref, qd_ref, qn_ref, latk_ref, dkvb_ref, selb_ref, winb_ref, pool_ref,
                        gates_ref, wuv_ref, dng_ref, lam_ref, exp_ref, fb_ref, fc_ref, posf_ref, o_ref,
                        ma_ref, la_ref, acca_ref, mb_ref, lb_ref, accb_ref, mc_ref, lc_ref, accc_ref,
                        *, kt, lam_init):
    i = pl.program_id(1)
    qb = qmla_ref.shape[1]
    seq = latk_ref.shape[0]
    nblk = pool_ref.shape[0]
    ra, rb, rc = A_HEADS * qb, 2 * B_HEADS * qb, C_HEADS * qb
    q0 = i * qb
    qa = qmla_ref[...].reshape(ra, 2 * LANE)
    qn = qn_ref[...].reshape(rc, LANE)
    qd_aug = jnp.concatenate([qd_ref[...].reshape(rb, LANE), fb_ref[...]], axis=1)
    qn_aug = jnp.concatenate([qn, fc_ref[...]], axis=1)

    def qpos_rows(rows, width):
        return q0 + _mod(lax.broadcasted_iota(jnp.int32, (rows, width), 0), qb)

    slope_c = _alibi_col(rc, qb, C_HEADS)

    kcv = pool_ref[...].astype(BF16)
    blk = lax.broadcasted_iota(jnp.int32, (rc, nblk), 1)
    qp = qpos_rows(rc, nblk)
    cvalid = blk < _div(qp, CMP_BLOCK)
    cdist = (qp - ((blk + 1) * CMP_BLOCK - 1)).astype(F32)
    sc = jnp.where(cvalid, _dot_t(qn, kcv) - slope_c * cdist, MASKED)
    mcmp = jnp.max(sc, axis=1, keepdims=True)
    pc = jnp.where(cvalid, jnp.exp2(sc - mcmp), 0.0)
    pc = pc / jnp.maximum(jnp.sum(pc, axis=1, keepdims=True), 1e-30)
    o_cmp = _dot(pc.astype(BF16), kcv)
    imp = jnp.sum(pc.reshape(C_HEADS, qb, nblk), axis=0)

    wlen = WINDOW + qb
    w0 = pl.multiple_of(jnp.maximum(q0 - WINDOW, 0), qb)
    wrows = winb_ref[pl.ds(w0, wlen), :]
    wd = qpos_rows(rc, wlen) - (w0 + lax.broadcasted_iota(jnp.int32, (rc, wlen), 1))
    sw = jnp.where((wd >= 0) & (wd <= WINDOW), _dot_t(qn, wrows) - slope_c * wd.astype(F32), MASKED)
    pw = jnp.exp2(sw - jnp.max(sw, axis=1, keepdims=True))
    o_win = _dot(pw.astype(BF16), wrows) / jnp.sum(pw, axis=1, keepdims=True)

    _softmax_init(ma_ref, la_ref, acca_ref)
    _softmax_init(mb_ref, lb_ref, accb_ref)
    _softmax_init(mc_ref, lc_ref, accc_ref)

    def tile_ab(j, diagonal):
        k0 = pl.multiple_of(j * kt, kt)
        lat = latk_ref[pl.ds(k0, kt), :]
        dk = dkvb_ref[pl.ds(k0, kt), :]
        sa = _dot_t(qa, lat)
        sb = _dot_t(qd_aug, jnp.concatenate([dk[:, 0:LANE], posf_ref[pl.ds(k0, kt), :]], axis=1))
        if diagonal:
            sa = jnp.where(k0 + lax.broadcasted_iota(jnp.int32, (ra, kt), 1) <= qpos_rows(ra, kt), sa, MASKED)
            sb = jnp.where(k0 + lax.broadcasted_iota(jnp.int32, (rb, kt), 1) <= qpos_rows(rb, kt), sb, MASKED)
        _softmax_step(sa, lambda p: _dot(p, lat[:, 0:LANE]), ma_ref, la_ref, acca_ref)
        _softmax_step(sb, lambda p: _dot(p, dk[:, LANE:2 * LANE]), mb_ref, lb_ref, accb_ref)

    def tile_c(j, diagonal, sel):
        k0 = pl.multiple_of(j * kt, kt)
        ks = selb_ref[pl.ds(k0, kt), :]
        ss = _dot_t(qn_aug, jnp.concatenate([ks, posf_ref[pl.ds(k0, kt), :]], axis=1))
        allowed = _dot(sel, exp_ref[j]) > 0.5
        if diagonal:
            qp1 = q0 + lax.broadcasted_iota(jnp.int32, (qb, kt), 0)
            kp1 = k0 + lax.broadcasted_iota(jnp.int32, (qb, kt), 1)
            allowed = allowed | ((_div(kp1, CMP_BLOCK) == _div(qp1, CMP_BLOCK)) & (kp1 <= qp1))
        ss = jnp.where(allowed[None], ss.reshape(C_HEADS, qb, kt), MASKED).reshape(rc, kt)
        _softmax_step(ss, lambda p: _dot(p, ks), mc_ref, lc_ref, accc_ref)

    n_full = q0 // kt
    tile_ab(n_full, True)
    sel = _topk_mask(imp, cvalid[:qb], N_SEL).astype(BF16)
    tile_c(n_full, True, sel)

    def full_tile(j, carry):
        tile_ab(j, False)
        tile_c(j, False, sel)
        return carry

    lax.fori_loop(0, n_full, full_tile, 0)

    o_lat = _softmax_out(la_ref, acca_ref).astype(BF16).reshape(A_HEADS, qb, LANE)
    for p in range(A_HEADS // 2):
        pair = jnp.concatenate([o_lat[2 * p], o_lat[2 * p + 1]], axis=-1)
        o_ref[:, p * LANE:(p + 1) * LANE] = _dot(pair, wuv_ref[p]).astype(o_ref.dtype)
    o_dif = _softmax_out(lb_ref, accb_ref).reshape(2 * B_HEADS, qb, LANE)
    lam = lam_ref[...]
    for h in range(B_HEADS):
        od = o_dif[2 * h] - lam * o_dif[2 * h + 1]
        od = _rms(od, dng_ref[...]) * (1.0 - lam_init)
        o_ref[:, MIX_A + h * LANE:MIX_A + (h + 1) * LANE] = od.astype(o_ref.dtype)
    o_sel = _softmax_out(lc_ref, accc_ref).reshape(C_HEADS, qb, LANE)
    o_cmp = o_cmp.reshape(C_HEADS, qb, LANE)
    o_win = o_win.reshape(C_HEADS, qb, LANE)
    gates = gates_ref[...]
    for h in range(C_HEADS):
        on = (gates[:, 3 * h:3 * h + 1] * o_cmp[h] + gates[:, 3 * h + 1:3 * h + 2] * o_sel[h]
              + gates[:, 3 * h + 2:3 * h + 3] * o_win[h])
        o_ref[:, MIX_A + MIX_B + h * LANE:MIX_A + MIX_B + (h + 1) * LANE] = on.astype(o_ref.dtype)


def _prompt_attn(pr, lw, lam_row, expand, feats, *, batch, seq, lam_init, qb=128, kt=512):
    kt = min(kt, seq)
    feat_b, feat_c, pos_f = feats
    nq = seq // qb
    nblk = seq // CMP_BLOCK
    hq = lambda b, i: (0, b * nq + i, 0)
    perb = lambda b, i: (b, 0)
    const2 = lambda b, i: (0, 0)
    const3 = lambda b, i: (0, 0, 0)
    ra, rb, rc = A_HEADS * qb, 2 * B_HEADS * qb, C_HEADS * qb
    stat = lambda r: [pltpu.VMEM((r, LANE), F32)] * 3
    return pl.pallas_call(
        functools.partial(_prompt_attn_kernel, kt=kt, lam_init=lam_init),
        grid=(batch, nq),
        in_specs=[
            pl.BlockSpec((A_HEADS, qb, 2 * LANE), hq),
            pl.BlockSpec((2 * B_HEADS, qb, LANE), hq),
            pl.BlockSpec((C_HEADS, qb, LANE), hq),
            pl.BlockSpec((seq, 2 * LANE), perb),
            pl.BlockSpec((seq, 4 * B_D), perb),
            pl.BlockSpec((seq, LANE), perb),
            pl.BlockSpec((seq, LANE), perb),
            pl.BlockSpec((nblk, LANE), perb),
            pl.BlockSpec((qb, LANE), lambda b, i: (b * nq + i, 0)),
            pl.BlockSpec((A_HEADS // 2, 2 * LANE, LANE), const3),
            pl.BlockSpec((1, LANE), const2),
            pl.BlockSpec((1, LANE), const2),
            pl.BlockSpec((seq // kt, nblk, kt), const3),
            pl.BlockSpec((rb, LANE), const2),
            pl.BlockSpec((rc, LANE), const2),
            pl.BlockSpec((seq, LANE), const2),
        ],
        out_specs=pl.BlockSpec((qb, MIX_W), lambda b, i: (b * nq + i, 0)),
        out_shape=jax.ShapeDtypeStruct((batch * seq, MIX_W), BF16),
        scratch_shapes=stat(ra) + stat(rb) + stat(rc),
        compiler_params=_cparams(("parallel", "arbitrary")),
        name="prompt_attn",
    )(pr["qmla"], pr["qd"], pr["qn"], pr["latk"], pr["dkvb"], pr["selb"], pr["winb"], pr["pool"],
      pr["gates"], lw["wuv_pair"], lw["diff_norm"], lam_row, expand, feat_b, feat_c, pos_f)


def _stream_pages(cache_ref, layer, pt_ref, ptn_ref, sem_ref, n_chunks, ppc, dst, consume):
    assert n_chunks % 2 == 0
    b = pl.program_id(0)
    nb = pl.num_programs(0)

    def page_copy(page, slot, p):
        return pltpu.make_async_copy(cache_ref.at[layer, page], dst(slot, p), sem_ref.at[slot])

    def start(table_ref, c, slot):
        for p in range(ppc):
            page_copy(table_ref[0, 0, c * ppc + p], slot, p).start()

    def wait(slot):
        for p in range(ppc):
            page_copy(0, slot, p).wait()

    @pl.when(b == 0)
    def _():
        start(pt_ref, 0, 0)

    def body(c, carry):
        slot = c % 2

        @pl.when(c + 1 < n_chunks)
        def _():
            start(pt_ref, c + 1, 1 - slot)

        @pl.when((c + 1 == n_chunks) & (b + 1 < nb))
        def _():
            start(ptn_ref, 0, 0)

        wait(slot)
        consume(c, slot)
        return carry

    lax.fori_loop(0, n_chunks, body, 0)


def _start_chunks(cache_ref, layer, table_ref, buf_ref, sem_ref, n_chunks, ppc):
    for c in range(n_chunks):
        for p in range(ppc):
            pltpu.make_async_copy(cache_ref.at[layer, table_ref[0, 0, c * ppc + p]], buf_ref.at[c, p],
                                  sem_ref.at[c]).start()


def _wait_chunk(cache_ref, layer, buf_ref, sem_ref, c, ppc):
    for p in range(ppc):
        pltpu.make_async_copy(cache_ref.at[layer, 0], buf_ref.at[c, p], sem_ref.at[c]).wait()


def _new_rows_tile(pad_ref, new_ref):
    pad_ref[...] = jnp.zeros(pad_ref.shape, F32)
    pad_ref[0:new_ref.shape[0], :] = new_ref[...]
    return pad_ref[...].astype(BF16)


def _dec_mla_kernel(pt_ref, ptn_ref, q_ref, new_ref, wuv_ref, cache_ref, o_ref,
                    buf_ref, sem_ref, pad_ref, m_ref, l_ref, acc_ref, *, layer, ppc):
    t = q_ref.shape[1]
    rows = A_HEADS * t
    n_pages = pt_ref.shape[2]
    q = q_ref[...].reshape(rows, 2 * LANE)[:, 0:A_LAT].astype(BF16)
    _softmax_init(m_ref, l_ref, acc_ref)

    def dst(slot, p):
        return buf_ref.at[slot, :, pl.ds(p * PAGE_SIZE, PAGE_SIZE)]

    def consume(c, slot):
        k_t = buf_ref[slot].astype(BF16)
        _softmax_step(_dot(q, k_t), lambda p: _dot_t(p, k_t[0:A_KVRANK, :]), m_ref, l_ref, acc_ref)

    _stream_pages(cache_ref, layer, pt_ref, ptn_ref, sem_ref, n_pages // ppc, ppc, dst, consume)
    kn = _new_rows_tile(pad_ref, new_ref)
    qj = _mod(lax.broadcasted_iota(jnp.int32, (rows, PAGE_SIZE), 0), t)
    kj = lax.broadcasted_iota(jnp.int32, (rows, PAGE_SIZE), 1)
    sn = jnp.where(kj <= qj, _dot_t(q, kn), MASKED)
    _softmax_step(sn, lambda p: _dot(p, kn[:, 0:A_KVRANK]), m_ref, l_ref, acc_ref)
    o_lat = _softmax_out(l_ref, acc_ref).astype(BF16)
    full = _dot(o_lat, wuv_ref[...]).reshape(A_HEADS, t, A_HEADS * A_VDIM)
    own = (_div(lax.broadcasted_iota(jnp.int32, (A_HEADS, 1, A_HEADS * A_VDIM), 2), A_VDIM)
           == lax.broadcasted_iota(jnp.int32, (A_HEADS, 1, A_HEADS * A_VDIM), 0))
    o_ref[...] = jnp.sum(jnp.where(own, full, 0.0), axis=0)


def _dec_diff_kernel(pt_ref, ptn_ref, q_ref, new_ref, dng_ref, lam_ref, cache_ref, o_ref,
                     buf_ref, sem_ref, pad_ref, m_ref, l_ref, acc_ref, *, layer, ppc, past, lam_init):
    t = q_ref.shape[1]
    rows = 2 * B_HEADS * t
    n_pages = pt_ref.shape[2]
    q = q_ref[...].reshape(rows, LANE).astype(BF16)
    slope = _alibi_col(rows, 2 * t, B_HEADS)
    _softmax_init(m_ref, l_ref, acc_ref)
    ck = ppc * PAGE_SIZE
    qpos = past + _mod(lax.broadcasted_iota(jnp.int32, (rows, ck), 0), t)

    def consume(c, slot):
        k = buf_ref[slot].reshape(ck, 4 * B_D).astype(BF16)
        dist = (qpos - (c * ck + lax.broadcasted_iota(jnp.int32, (rows, ck), 1))).astype(F32)
        _softmax_step(_dot_t(q, k[:, 0:LANE]) - slope * dist, lambda p: _dot(p, k[:, LANE:2 * LANE]),
                      m_ref, l_ref, acc_ref)

    _stream_pages(cache_ref, layer, pt_ref, ptn_ref, sem_ref, n_pages // ppc, ppc,
                  lambda slot, p: buf_ref.at[slot, p], consume)
    kn = _new_rows_tile(pad_ref, new_ref)
    dj = (_mod(lax.broadcasted_iota(jnp.int32, (rows, PAGE_SIZE), 0), t)
          - lax.broadcasted_iota(jnp.int32, (rows, PAGE_SIZE), 1))
    sn = jnp.where(dj >= 0, _dot_t(q, kn[:, 0:LANE]) - slope * dj.astype(F32), MASKED)
    _softmax_step(sn, lambda p: _dot(p, kn[:, LANE:2 * LANE]), m_ref, l_ref, acc_ref)
    o = _softmax_out(l_ref, acc_ref)
    lam = lam_ref[...]
    for h in range(B_HEADS):
        od = o[2 * h * t:(2 * h + 1) * t] - lam * o[(2 * h + 1) * t:(2 * h + 2) * t]
        o_ref[:, h * LANE:(h + 1) * LANE] = _rms(od, dng_ref[...]) * (1.0 - lam_init)


def _dec_nsa_kernel(pt_ref, ptn_ref, q_ref, seln_ref, winn_ref, wst_ref, gates_ref, wp_ref, exp_ref,
                    fq_ref, pfs_ref, cmp_cache_ref, sel_cache_ref, o_ref, nwin_ref,
                    bufc_ref, semc_ref, bufs_ref, sems_ref, pad_ref, pool_ref, wbuf_ref,
                    m_ref, l_ref, acc_ref, *, layer, ppc, past):
    t = q_ref.shape[1]
    rows = C_HEADS * t
    n_pages = pt_ref.shape[2]
    n_chunks = n_pages // ppc
    ck = ppc * PAGE_SIZE
    bpc = ck // CMP_BLOCK
    nblk = n_chunks * bpc
    q = q_ref[...].reshape(rows, LANE).astype(BF16)
    slope = _alibi_col(rows, t, C_HEADS)

    b = pl.program_id(0)

    @pl.when(b == 0)
    def _():
        _start_chunks(cmp_cache_ref, layer, pt_ref, bufc_ref, semc_ref, n_chunks, ppc)

    _start_chunks(sel_cache_ref, layer, pt_ref, bufs_ref, sems_ref, n_chunks, ppc)
    for c in range(n_chunks):
        _wait_chunk(cmp_cache_ref, layer, bufc_ref, semc_ref, c, ppc)
        r = bufc_ref[c].reshape(bpc, CMP_BLOCK, LANE) * wp_ref[...][None]
        pool_ref[c * bpc:(c + 1) * bpc, :] = jnp.sum(r, axis=1)

    @pl.when(b + 1 < pl.num_programs(0))
    def _():
        _start_chunks(cmp_cache_ref, layer, ptn_ref, bufc_ref, semc_ref, n_chunks, ppc)

    kcv = pool_ref[...].astype(BF16)
    blk = lax.broadcasted_iota(jnp.int32, (rows, nblk), 1)
    qp = past + _mod(lax.broadcasted_iota(jnp.int32, (rows, nblk), 0), t)
    cvalid = blk < _div(qp, CMP_BLOCK)
    cdist = (qp - ((blk + 1) * CMP_BLOCK - 1)).astype(F32)
    sc = jnp.where(cvalid, _dot_t(q, kcv) - slope * cdist, MASKED)
    pc = jnp.where(cvalid, jnp.exp2(sc - jnp.max(sc, axis=1, keepdims=True)), 0.0)
    pc = pc / jnp.maximum(jnp.sum(pc, axis=1, keepdims=True), 1e-30)
    o_cmp = _dot(pc.astype(BF16), kcv)
    imp = jnp.sum(pc.reshape(C_HEADS, t, nblk), axis=0)
    sel = _topk_mask_by_rank(imp, cvalid[:t], N_SEL)
    sel_rows = jnp.tile(sel, (C_HEADS, 1))

    _softmax_init(m_ref, l_ref, acc_ref)
    q_aug = jnp.concatenate([q, fq_ref[...]], axis=1)

    for c in range(n_chunks):
        _wait_chunk(sel_cache_ref, layer, bufs_ref, sems_ref, c, ppc)
        k = bufs_ref[c].reshape(ck, LANE).astype(BF16)
        chosen = _dot(sel_rows[:, c * bpc:(c + 1) * bpc].astype(BF16), exp_ref[...]) > 0.5
        s = jnp.where(chosen, _dot_t(q_aug, jnp.concatenate([k, pfs_ref[c]], axis=1)), MASKED)
        _softmax_step(s, lambda p, k=k: _dot(p, k), m_ref, l_ref, acc_ref)
    kn = _new_rows_tile(pad_ref, seln_ref)
    kj = lax.broadcasted_iota(jnp.int32, (rows, PAGE_SIZE), 1)
    qj = _mod(lax.broadcasted_iota(jnp.int32, (rows, PAGE_SIZE), 0), t)
    sn = jnp.where(kj <= qj, _dot_t(q, kn) + slope * kj.astype(F32), MASKED)
    _softmax_step(sn, lambda p: _dot(p, kn), m_ref, l_ref, acc_ref)
    o_sel = _softmax_out(l_ref, acc_ref)

    wb = wst_ref.shape[0]
    wbuf_ref[...] = jnp.zeros(wbuf_ref.shape, F32)
    wbuf_ref[0:wb, :] = wst_ref[...]
    wbuf_ref[wb:wb + t, :] = winn_ref[...]
    wrows = wbuf_ref[...].astype(BF16)
    wl = wbuf_ref.shape[0]
    idx = lax.broadcasted_iota(jnp.int32, (rows, wl), 1)
    wpos = jnp.where(idx < wb, past - wb + idx, past + idx - wb)
    wd = past + _mod(lax.broadcasted_iota(jnp.int32, (rows, wl), 0), t) - wpos
    ok = (wd >= 0) & (wd <= WINDOW) & (idx < wb + t) & (wpos >= 0)
    sw = jnp.where(ok, _dot_t(q, wrows) - slope * wd.astype(F32), MASKED)
    pw = jnp.exp2(sw - jnp.max(sw, axis=1, keepdims=True))
    o_win = _dot(pw.astype(BF16), wrows) / jnp.sum(pw, axis=1, keepdims=True)
    nwin_ref[0:wb - t, :] = wst_ref[t:wb, :]
    nwin_ref[wb - t:wb, :] = winn_ref[...]

    gates = gates_ref[...]
    for h in range(C_HEADS):
        r = slice(h * t, (h + 1) * t)
        o_ref[:, h * LANE:(h + 1) * LANE] = (gates[:, 3 * h:3 * h + 1] * o_cmp[r]
                                              + gates[:, 3 * h + 1:3 * h + 2] * o_sel[r]
                                              + gates[:, 3 * h + 2:3 * h + 3] * o_win[r])


def _sample_attn(sp, lw, lam_row, expand_s, feats, caches, win_state, page_table3, *, layer, lam_init,
                 past, ppc):
    cache_mla, cache_diff, cache_cmp, cache_sel = caches
    feat_s, pos_s = feats
    db, _, n_pages = page_table3.shape
    ts = sp["lat"].shape[0]
    t = ts // db
    wb = win_state.shape[2]
    pt_spec = pl.BlockSpec((1, 1, n_pages), lambda b: (b, 0, 0), memory_space=pltpu.SMEM)
    ptn_spec = pl.BlockSpec((1, 1, n_pages), lambda b: (jnp.minimum(b + 1, db - 1), 0, 0),
                            memory_space=pltpu.SMEM)
    hq = lambda b: (0, b, 0)
    row = lambda b: (b, 0)
    const2 = lambda b: (0, 0)
    any_spec = pl.BlockSpec(memory_space=pl.ANY)
    stat = lambda r: [pltpu.VMEM((r, LANE), F32)] * 3
    ck = ppc * PAGE_SIZE
    dma2 = pltpu.SemaphoreType.DMA((2,))

    def stream_scratch(width):
        return [pltpu.VMEM((2, ppc, PAGE_SIZE, width), F32), dma2]

    o_a = pl.pallas_call(
        functools.partial(_dec_mla_kernel, layer=layer, ppc=ppc),
        grid=(db,),
        in_specs=[pt_spec, ptn_spec,
                  pl.BlockSpec((A_HEADS, t, 2 * LANE), hq),
                  pl.BlockSpec((t, A_LAT), row),
                  pl.BlockSpec((A_KVRANK, A_HEADS * A_VDIM), const2),
                  any_spec],
        out_specs=pl.BlockSpec((t, MIX_A), row),
        out_shape=jax.ShapeDtypeStruct((ts, MIX_A), F32),
        scratch_shapes=([pltpu.VMEM((2, A_LAT, ck), F32), dma2, pltpu.VMEM((PAGE_SIZE, A_LAT), F32)]
                        + stat(A_HEADS * t)),
        compiler_params=_cparams(("arbitrary",)),
        name="sample_mla",
    )(page_table3, page_table3, sp["qmla"], sp["lat"], lw["wuv_flat"], cache_mla)

    o_b = pl.pallas_call(
        functools.partial(_dec_diff_kernel, layer=layer, ppc=ppc, past=past, lam_init=lam_init),
        grid=(db,),
        in_specs=[pt_spec, ptn_spec,
                  pl.BlockSpec((2 * B_HEADS, t, LANE), hq),
                  pl.BlockSpec((t, 4 * B_D), row),
                  pl.BlockSpec((1, LANE), const2),
                  pl.BlockSpec((1, LANE), const2),
                  any_spec],
        out_specs=pl.BlockSpec((t, MIX_B), row),
        out_shape=jax.ShapeDtypeStruct((ts, MIX_B), F32),
        scratch_shapes=(stream_scratch(4 * B_D) + [pltpu.VMEM((PAGE_SIZE, 4 * B_D), F32)]
                        + stat(2 * B_HEADS * t)),
        compiler_params=_cparams(("arbitrary",)),
        name="sample_diff",
    )(page_table3, page_table3, sp["qd"], sp["dkv"], lw["diff_norm"], lam_row, cache_diff)

    n_chunks = n_pages // ppc
    bpc = ppc * PAGE_SIZE // CMP_BLOCK
    rows_c = C_HEADS * t
    o_c, new_win = pl.pallas_call(
        functools.partial(_dec_nsa_kernel, layer=layer, ppc=ppc, past=past),
        grid=(db,),
        in_specs=[pt_spec, ptn_spec,
                  pl.BlockSpec((C_HEADS, t, LANE), hq),
                  pl.BlockSpec((t, LANE), row),
                  pl.BlockSpec((t, LANE), row),
                  pl.BlockSpec((None, None, wb, LANE), lambda b: (layer, b, 0, 0)),
                  pl.BlockSpec((t, LANE), row),
                  pl.BlockSpec((CMP_BLOCK, LANE), const2),
                  pl.BlockSpec((bpc, ppc * PAGE_SIZE), const2),
                  pl.BlockSpec((rows_c, LANE), const2),
                  pl.BlockSpec((n_chunks, ck, LANE), lambda b: (0, 0, 0), pipeline_mode=pl.Buffered(1)),
                  any_spec, any_spec],
        out_specs=[pl.BlockSpec((t, MIX_C), row),
                   pl.BlockSpec((None, wb, LANE), lambda b: (b, 0, 0))],
        out_shape=[jax.ShapeDtypeStruct((ts, MIX_C), F32),
                   jax.ShapeDtypeStruct((db, wb, LANE), F32)],
        scratch_shapes=(2 * [pltpu.VMEM((n_chunks, ppc, PAGE_SIZE, LANE), F32),
                             pltpu.SemaphoreType.DMA((n_chunks,))]
                        + [pltpu.VMEM((PAGE_SIZE, LANE), F32),
                           pltpu.VMEM((n_chunks * bpc, LANE), F32),
                           pltpu.VMEM((wb + PAGE_SIZE, LANE), F32)]
                        + stat(rows_c)),
        compiler_params=_cparams(("arbitrary",)),
        name="sample_nsa",
    )(page_table3, page_table3, sp["qn"], sp["sel"], sp["win"], win_state, sp["gates"], lw["w_pool"],
      expand_s, feat_s, pos_s, cache_cmp, cache_sel)
    mixed = jnp.concatenate([o_a, o_b, o_c], axis=-1).astype(BF16)
    return mixed, new_win


def _xattn_kernel(x_ref, g_ref, wq_ref, mem_ref, wo_ref, o_ref, *, small):
    grp, tq, d = x_ref.shape
    x = x_ref[...].reshape(grp * tq, d)
    q = _dot(_rms(x, g_ref[...]).astype(BF16), wq_ref[...]) * (M_DH ** -0.5)
    hd = M_HEADS * M_DH
    outs = []
    for gi in range(grp):
        kv = mem_ref[gi]
        qg = q[gi * tq:(gi + 1) * tq]
        if not small:
            kv = kv.astype(BF16)
            qg = qg.astype(BF16)
        heads = []
        for h in range(M_HEADS):
            s = _dot_t(qg[:, h * M_DH:(h + 1) * M_DH], kv[:, h * M_DH:(h + 1) * M_DH])
            p = jnp.exp(s - jnp.max(s, axis=1, keepdims=True))
            inv = 1.0 / jnp.sum(p, axis=1, keepdims=True)
            if not small:
                p = p.astype(BF16)
            heads.append(_dot(p, kv[:, hd + h * M_DH:hd + (h + 1) * M_DH]) * inv)
        outs.append(jnp.concatenate(heads, axis=-1))
    attn = outs[0] if grp == 1 else jnp.concatenate(outs, axis=0)
    o_ref[...] = (x + _dot(attn.astype(BF16), wo_ref[...])).reshape(grp, tq, d)


def _xattn(x3, g, wq, mem, wo, *, layer, grp, tq):
    nb, t, d = x3.shape
    nm = mem.shape[-2]
    hd2 = mem.shape[-1]
    if mem.ndim == 4:
        mem_spec = pl.BlockSpec((None, grp, nm, hd2), lambda b, i: (layer, b, 0, 0))
    else:
        mem_spec = pl.BlockSpec((grp, nm, hd2), lambda b, i: (b, 0, 0))
    return pl.pallas_call(
        functools.partial(_xattn_kernel, small=tq < 16),
        grid=(nb // grp, t // tq),
        in_specs=[pl.BlockSpec((grp, tq, d), lambda b, i: (b, i, 0)),
                  pl.BlockSpec((1, d), lambda b, i: (0, 0)),
                  pl.BlockSpec((d, M_HEADS * M_DH), lambda b, i: (0, 0)),
                  mem_spec,
                  pl.BlockSpec((M_HEADS * M_DH, d), lambda b, i: (0, 0))],
        out_specs=pl.BlockSpec((grp, tq, d), lambda b, i: (b, i, 0)),
        out_shape=jax.ShapeDtypeStruct((nb, t, d), F32),
        compiler_params=_cparams(("parallel", "parallel")),
        name="xattn",
    )(x3, g, wq, mem, wo)


def _wall_columns():
    offs = np.concatenate([[0], np.cumsum(IN_SIZES)])
    o_qc, o_ckv, o_kr, o_qb, o_dkv, o_qn, o_cmp, o_sel, o_win, o_gl = offs[:-1]
    idx = np.zeros(N_WALL, np.int32)
    sgn = np.zeros(N_WALL, np.float32)

    def put(dst, src, sign=1.0):
        src = np.asarray(src)
        idx[dst:dst + len(src)] = src
        sgn[dst:dst + len(src)] = sign

    put(S_QC, o_qc + np.arange(A_QRANK))
    put(S_CKV, o_ckv + np.arange(A_KVRANK))
    for h in range(B_HEADS):
        put(S_QB + (2 * h) * LANE, o_qb + h * 2 * B_D + np.arange(B_D))
        put(S_QB + (2 * h + 1) * LANE + B_D, o_qb + h * 2 * B_D + B_D + np.arange(B_D))
    put(S_DKV, o_dkv + np.arange(4 * B_D))
    for h in range(C_HEADS):
        put(S_QN + h * LANE, o_qn + h * C_DK + np.arange(C_DK))
    put(S_CMP, o_cmp + np.arange(LANE))
    put(S_SEL, o_sel + np.arange(LANE))
    put(S_WIN, o_win + np.arange(LANE))
    half = A_ROPE // 2
    put(S_KR, o_kr + np.arange(A_ROPE))
    put(S_KRR, o_kr + half + np.arange(half), -1.0)
    put(S_KRR + half, o_kr + np.arange(half))
    put(S_GL, o_gl + np.arange(3 * C_HEADS))
    return idx, sgn


def _uq_columns():
    width = A_HEADS * LANE
    per = A_NOPE + A_ROPE
    half = A_ROPE // 2
    idx = np.zeros((3, width), np.int32)
    sgn = np.zeros((3, width), np.float32)
    for h in range(A_HEADS):
        idx[0, h * LANE:h * LANE + A_NOPE] = h * per + np.arange(A_NOPE)
        sgn[0, h * LANE:h * LANE + A_NOPE] = 1.0
        idx[1, h * LANE:h * LANE + A_ROPE] = h * per + A_NOPE + np.arange(A_ROPE)
        sgn[1, h * LANE:h * LANE + A_ROPE] = 1.0
        idx[2, h * LANE:h * LANE + half] = h * per + A_NOPE + half + np.arange(half)
        sgn[2, h * LANE:h * LANE + half] = -1.0
        idx[2, h * LANE + half:h * LANE + A_ROPE] = h * per + A_NOPE + np.arange(half)
        sgn[2, h * LANE + half:h * LANE + A_ROPE] = 1.0
    return idx, sgn


def _wout_rows():
    idx = np.zeros(MIX_W, np.int32)
    sgn = np.zeros(MIX_W, np.float32)
    idx[:MIX_A + MIX_B] = np.arange(MIX_A + MIX_B)
    sgn[:MIX_A + MIX_B] = 1.0
    for h in range(C_HEADS):
        dst = MIX_A + MIX_B + h * LANE + C_DK
        idx[dst:dst + C_DV] = MIX_A + MIX_B + h * C_DV + np.arange(C_DV)
        sgn[dst:dst + C_DV] = 1.0
    return idx, sgn


def _gather_cols(w, idx, sgn):
    return (jnp.take(w, jnp.asarray(idx), axis=1) * jnp.asarray(sgn)[None, :]).astype(BF16)


def _layer_weights(l, p):
    d = p["w_in"].shape[1]
    lw = {}
    lw["mix_norm"] = p["mix_norm"][l].reshape(1, d)
    lw["w_all"] = _gather_cols(p["w_in"][l], *_wall_columns())
    lw["q_norm"] = p["mla_q_norm"][l].reshape(1, A_QRANK)
    uidx, usgn = _uq_columns()
    lw["wn"], lw["wr"], lw["wrr"] = (_gather_cols(p["mla_w_uq"][l], uidx[k], usgn[k]) for k in range(3))
    wuk = jnp.transpose(p["mla_w_uk"][l], (1, 2, 0))
    lw["wuk"] = jnp.pad(wuk, ((0, 0), (0, LANE - A_NOPE), (0, 0))).astype(BF16)
    lw["kv_norm"] = p["mla_kv_norm"][l].reshape(1, A_KVRANK)
    wuv = p["mla_w_uv"][l]
    lw["wuv_flat"] = wuv.reshape(A_KVRANK, A_HEADS * A_VDIM).astype(BF16)
    z = jnp.zeros((A_KVRANK, A_VDIM), F32)
    pairs = [jnp.concatenate([jnp.concatenate([wuv[:, 2 * j], z], axis=1),
                              jnp.concatenate([z, wuv[:, 2 * j + 1]], axis=1)], axis=0)
             for j in range(A_HEADS // 2)]
    lw["wuv_pair"] = jnp.stack(pairs).astype(BF16)
    lw["diff_norm"] = p["diff_norm"][l].reshape(1, 2 * B_D)
    lw["w_pool"] = jnp.broadcast_to(p["nsa_w_cmp"][l][:, None], (CMP_BLOCK, LANE))
    ridx, rsgn = _wout_rows()
    lw["w_out"] = (jnp.take(p["w_out"][l], jnp.asarray(ridx), axis=0) * jnp.asarray(rsgn)[:, None]).astype(BF16)
    for k in ("ffn1_norm", "ffn2_norm", "xattn_norm", "mem_norm"):
        lw[k] = p[k][l].reshape(1, d)
    for k in ("ffn1_wi", "ffn1_wo", "ffn2_wi", "ffn2_wo", "xattn_wq", "xattn_wkv", "xattn_wo"):
        lw[k] = p[k][l].astype(BF16)
    return lw


def _rope_tables(pos):
    half = A_ROPE // 2
    inv = ROPE_BASE ** (-jnp.arange(half, dtype=F32) / half)
    ang = pos.astype(F32)[:, None] * inv
    pad = jnp.zeros((pos.shape[0], LANE - A_ROPE), F32)
    cos_t = jnp.concatenate([jnp.cos(ang), jnp.cos(ang), pad], axis=1)
    sin_t = jnp.concatenate([jnp.sin(ang), jnp.sin(ang), pad], axis=1)
    return cos_t, sin_t


def _slope_feats(n_heads, tiles_per_head, rows_per_tile):
    c = LOG2E * jnp.exp2(-(8.0 / n_heads) * (jnp.arange(n_heads, dtype=F32) + 1.0))
    c1 = c.astype(BF16).astype(F32)
    c2 = (c - c1).astype(BF16).astype(F32)
    c3 = (c - c1 - c2).astype(BF16).astype(F32)
    row = jnp.stack([c1, c2, c3, c1, c2, c3], axis=1)
    row = jnp.pad(row, ((0, 0), (0, LANE - 6)))
    return jnp.repeat(row, tiles_per_head * rows_per_tile, axis=0).astype(BF16)


def _pos_feats(pos):
    pos = np.asarray(pos, np.int64)
    lo = pos % CMP_BLOCK
    hi = pos - lo
    assert np.all(np.abs(hi) // CMP_BLOCK <= 256)
    f = np.zeros((len(pos), LANE), np.float32)
    f[:, 0:3] = hi[:, None]
    f[:, 3:6] = lo[:, None]
    return jnp.asarray(f, BF16)


def _block_expander(n_blocks, n_keys):
    e = (np.arange(n_keys)[None, :] // CMP_BLOCK) == np.arange(n_blocks)[:, None]
    return e.astype(np.float32)


def _tile(n, pref):
    t = min(n, pref)
    assert n % t == 0, (n, t)
    return t


def kernel(x_prompt, x_sample, mem_prompt, cache_mla, cache_diff, cache_nsa_cmp, cache_nsa_sel, state_nsa_win, cache_mem, page_table, ffn1_norm, ffn1_wi, ffn1_wo, mix_norm, w_in, mla_q_norm, mla_w_uq, mla_kv_norm, mla_w_uk, mla_w_uv, diff_lam, diff_norm, nsa_w_cmp, w_out, xattn_norm, mem_norm, xattn_wq, xattn_wkv, xattn_wo, ffn2_norm, ffn2_wi, ffn2_wo, final_norm):
    params = dict(ffn1_norm=ffn1_norm, ffn1_wi=ffn1_wi, ffn1_wo=ffn1_wo, mix_norm=mix_norm, w_in=w_in,
                  mla_q_norm=mla_q_norm, mla_w_uq=mla_w_uq, mla_kv_norm=mla_kv_norm, mla_w_uk=mla_w_uk,
                  mla_w_uv=mla_w_uv, diff_norm=diff_norm, nsa_w_cmp=nsa_w_cmp, w_out=w_out,
                  xattn_norm=xattn_norm, mem_norm=mem_norm, xattn_wq=xattn_wq, xattn_wkv=xattn_wkv,
                  xattn_wo=xattn_wo, ffn2_norm=ffn2_norm, ffn2_wi=ffn2_wi, ffn2_wo=ffn2_wo)
    batch, seq, d = x_prompt.shape
    db, ds, _ = x_sample.shape
    depth = w_in.shape[0]
    n_mem = mem_prompt.shape[1]
    n_pages = page_table.shape[1]
    past = n_pages * PAGE_SIZE
    tp, ts = batch * seq, db * ds
    assert ds % SUBLANE == 0 and seq % 128 == 0 and state_nsa_win.shape[2] == min(WINDOW, past)
    tm_p = _tile(seq, 512)
    tm_s = _tile(ts, 256)
    assert tm_s % ds == 0
    tf = _tile(ffn1_wo.shape[1], 512)
    tn = _tile(d, 1024)

    cos_p, sin_p = _rope_tables(jnp.arange(seq))
    cos_s, sin_s = _rope_tables(past + (jnp.arange(tm_s) % ds))
    kt = min(512, seq)
    expand_p = jnp.asarray(_block_expander(seq // CMP_BLOCK, seq).reshape(seq // CMP_BLOCK, seq // kt, kt)
                           .transpose(1, 0, 2), BF16)
    feats_p = (_slope_feats(B_HEADS, 2, 128), _slope_feats(C_HEADS, 1, 128), _pos_feats(np.arange(seq)))
    ppc = min(32, n_pages // 2)
    assert n_pages % (2 * ppc) == 0
    cache_mla_t = jnp.swapaxes(cache_mla, 2, 3)
    expand_s = jnp.asarray(_block_expander(ppc * PAGE_SIZE // CMP_BLOCK, ppc * PAGE_SIZE), BF16)
    feats_s = (_slope_feats(C_HEADS, 1, ds),
               _pos_feats(np.arange(past) - past).reshape(n_pages // ppc, ppc * PAGE_SIZE, LANE))
    page_table3 = page_table.reshape(db, 1, n_pages)
    fg = final_norm.reshape(1, d)

    xp = x_prompt.reshape(tp, d)
    xs = x_sample.reshape(ts, d)
    mem2 = mem_prompt.reshape(batch * n_mem, d)
    outs = {k: [] for k in ("mla_p", "mla_s", "diff_p", "diff_s", "cmp_p", "cmp_s", "sel_p", "sel_s",
                            "win_p", "win_s", "mem_p")}
    for l in range(depth):
        lw = _layer_weights(l, params)
        lam_init = 0.8 - 0.6 * math.exp(-0.3 * l)
        dl = diff_lam[l].astype(F32)
        lam = jnp.exp(jnp.sum(dl[0] * dl[1])) - jnp.exp(jnp.sum(dl[2] * dl[3])) + lam_init
        lam_row = jnp.full((1, LANE), lam, F32)
        xp = _ffn(xp, lw["ffn1_norm"], lw["ffn1_wi"], lw["ffn1_wo"], fg, final_norm=False, tm=tm_p, tf=tf)
        xs = _ffn(xs, lw["ffn1_norm"], lw["ffn1_wi"], lw["ffn1_wo"], fg, final_norm=False, tm=tm_s, tf=tf)
        pr = _project(xp, lw, cos_p, sin_p, tm=tm_p, table_blocks=seq // tm_p, qdt=BF16)
        mix_p = _prompt_attn(pr, lw, lam_row, expand_p, feats_p, batch=batch, seq=seq, lam_init=lam_init,
                             kt=kt)
        xp = _matmul(mix_p, lw["w_out"], res=xp, tm=tm_p, tn=tn, name="w_out")
        sp = _project(xs, lw, cos_s, sin_s, tm=tm_s, table_blocks=1, qdt=F32)
        mix_s, new_win = _sample_attn(sp, lw, lam_row, expand_s, feats_s,
                                      (cache_mla_t, cache_diff, cache_nsa_cmp, cache_nsa_sel),
                                      state_nsa_win, page_table3, layer=l, lam_init=lam_init, past=past,
                                      ppc=ppc)
        xs = _matmul(mix_s, lw["w_out"], res=xs, tm=tm_s, tn=tn, name="w_out_s")
        mem_kv = _matmul(mem2, lw["xattn_wkv"], g=lw["mem_norm"], tm=_tile(batch * n_mem, 512),
                         tn=_tile(lw["xattn_wkv"].shape[1], 1024), name="mem_kv")
        mem_kv = mem_kv.reshape(batch, n_mem, -1)
        xp = _xattn(xp.reshape(batch, seq, d), lw["xattn_norm"], lw["xattn_wq"], mem_kv, lw["xattn_wo"],
                    layer=l, grp=1, tq=tm_p).reshape(tp, d)
        xs = _xattn(xs.reshape(db, ds, d), lw["xattn_norm"], lw["xattn_wq"], cache_mem, lw["xattn_wo"],
                    layer=l, grp=_tile(db, 8), tq=ds).reshape(ts, d)
        last = l == depth - 1
        xp = _ffn(xp, lw["ffn2_norm"], lw["ffn2_wi"], lw["ffn2_wo"], fg, final_norm=last, tm=tm_p, tf=tf)
        xs = _ffn(xs, lw["ffn2_norm"], lw["ffn2_wi"], lw["ffn2_wo"], fg, final_norm=last, tm=tm_s, tf=tf)
        wlen = min(WINDOW, seq)
        outs["mla_p"].append(pr["lat"].reshape(batch, seq, A_LAT))
        outs["mla_s"].append(sp["lat"].reshape(db, ds, A_LAT))
        outs["diff_p"].append(pr["dkv"].reshape(batch, seq, 4 * B_D))
        outs["diff_s"].append(sp["dkv"].reshape(db, ds, 4 * B_D))
        outs["cmp_p"].append(pr["cmp"].reshape(batch, seq, LANE))
        outs["cmp_s"].append(sp["cmp"].reshape(db, ds, LANE))
        outs["sel_p"].append(pr["sel"].reshape(batch, seq, LANE))
        outs["sel_s"].append(sp["sel"].reshape(db, ds, LANE))
        outs["win_p"].append(pr["win"].reshape(batch, seq, LANE)[:, seq - wlen:])
        outs["win_s"].append(new_win)
        outs["mem_p"].append(mem_kv)
    st = {k: jnp.stack(v) for k, v in outs.items()}
    return (xp.reshape(batch, seq, d), xs.reshape(db, ds, d),
            st["mla_p"], st["mla_s"], st["diff_p"], st["diff_s"], st["cmp_p"], st["cmp_s"],
            st["sel_p"], st["sel_s"], st["win_p"], st["win_s"], st["mem_p"])
```

```python
import functools
import math

import numpy as np
import jax
import jax.numpy as jnp
from jax import lax
from jax.experimental import pallas as pl
from jax.experimental.pallas import tpu as pltpu

F32 = jnp.float32
BF16 = jnp.bfloat16

A_HEADS, A_NOPE, A_ROPE, A_VDIM, A_QRANK, A_KVRANK = 16, 64, 32, 64, 384, 128
A_LAT = A_KVRANK + A_ROPE
ROPE_BASE = 10000.0
B_HEADS, B_D = 4, 64
C_HEADS, C_DK, C_DV = 8, 64, 64
CMP_BLOCK, N_SEL, WINDOW = 64, 16, 512
M_HEADS, M_DH = 4, 128
PAGE_SIZE = 128
EPS = 1e-6
IN_SIZES = (A_QRANK, A_KVRANK, A_ROPE, B_HEADS * 2 * B_D, 4 * B_D, C_HEADS * C_DK,
            C_DK + C_DV, C_DK + C_DV, C_DK + C_DV, 3 * C_HEADS)

LANE = 128
SUBLANE = 8
VMEM_LIMIT = 56 * 1024 * 1024

MASKED = -2e30
M_INIT = -1e30
LOG2E = math.log2(math.e)

S_QC = 0
S_CKV = S_QC + A_QRANK
S_QB = S_CKV + LANE
S_DKV = S_QB + 2 * B_HEADS * LANE
S_QN = S_DKV + 4 * B_D
S_CMP = S_QN + C_HEADS * LANE
S_SEL = S_CMP + LANE
S_WIN = S_SEL + LANE
S_KR = S_WIN + LANE
S_KRR = S_KR + LANE
S_GL = S_KRR + LANE
N_WALL = S_GL + LANE

MIX_A = A_HEADS * A_VDIM
MIX_B = B_HEADS * LANE
MIX_C = C_HEADS * LANE
MIX_W = MIX_A + MIX_B + MIX_C


def _cparams(sem):
    return pltpu.CompilerParams(dimension_semantics=sem, vmem_limit_bytes=VMEM_LIMIT)


def _rms(x, g):
    return x * lax.rsqrt(jnp.mean(x * x, axis=-1, keepdims=True) + EPS) * g


def _dot(a, b):
    return jnp.dot(a, b, preferred_element_type=F32)


def _dot_t(a, b):
    return lax.dot_general(a, b, (((1,), (1,)), ((), ())), preferred_element_type=F32)


def _rep(x, n):
    if n == LANE:
        return x
    if n < LANE:
        return x[:, :n]
    return jnp.tile(x, (1, n // LANE))


def _log2(n):
    assert n > 0 and n & (n - 1) == 0, n
    return n.bit_length() - 1


def _div(x, n):
    return lax.shift_right_logical(x, _log2(n))


def _mod(x, n):
    return x & (n - 1)


def _alibi_col(rows, rows_per_head, n_heads):
    h = _div(lax.broadcasted_iota(jnp.int32, (rows, 1), 0), rows_per_head)
    return LOG2E * jnp.exp2(-(8.0 / n_heads) * (h + 1).astype(F32))


def _softmax_step(s, pv, m_ref, l_ref, acc_ref):
    kt = s.shape[1]
    m_prev = m_ref[...]
    m_next = jnp.maximum(m_prev, jnp.max(s, axis=1, keepdims=True))
    p = jnp.exp2(s - _rep(m_next, kt))
    alpha = jnp.exp2(m_prev - m_next)
    l_ref[...] = alpha * l_ref[...] + jnp.sum(p, axis=1, keepdims=True)
    m_ref[...] = m_next
    acc_ref[...] = acc_ref[...] * _rep(alpha, acc_ref.shape[1]) + pv(p.astype(BF16))


def _softmax_init(m_ref, l_ref, acc_ref):
    m_ref[...] = jnp.full(m_ref.shape, M_INIT, F32)
    l_ref[...] = jnp.zeros(l_ref.shape, F32)
    acc_ref[...] = jnp.zeros(acc_ref.shape, F32)


def _softmax_out(l_ref, acc_ref):
    inv = 1.0 / jnp.maximum(l_ref[...], 1e-30)
    return acc_ref[...] * _rep(inv, acc_ref.shape[1])


def _topk_mask(imp, valid, k):
    n = imp.shape[1]
    lane = lax.broadcasted_iota(jnp.int32, imp.shape, 1).astype(F32)
    work = jnp.where(valid, imp, -1.0)
    sel = jnp.zeros(imp.shape, F32)
    for _ in range(min(k, n)):
        mx = jnp.max(work, axis=1, keepdims=True)
        idx = jnp.min(jnp.where(work == mx, lane, float(n)), axis=1, keepdims=True)
        hit = lane == idx
        sel = jnp.where(hit & (mx >= 0.0), 1.0, sel)
        work = jnp.where(hit, -2.0, work)
    return sel


def _topk_mask_by_rank(imp, valid, k):
    r, n = imp.shape
    x = jnp.where(valid, imp, -1.0)
    jj = lax.broadcasted_iota(jnp.int32, (n, n), 0)
    cc = lax.broadcasted_iota(jnp.int32, (n, n), 1)
    eye = jnp.where(jj == cc, 1.0, 0.0)
    lower = jnp.where(jj < cc, 1.0, 0.0)
    rows = []
    for i in range(r):
        row = x[i:i + 1, :]
        col = jnp.sum(eye * row, axis=1, keepdims=True)
        beats = jnp.where(col > row, 1.0, jnp.where(col == row, lower, 0.0))
        rank = jnp.sum(beats, axis=0, keepdims=True)
        rows.append(jnp.where((rank < float(k)) & (row >= 0.0), 1.0, 0.0))
    return jnp.concatenate(rows, axis=0)


def _ffn_kernel(x_ref, g_ref, wg_ref, wu_ref, wo_ref, fg_ref, o_ref, xn_ref, acc_ref, *, final_norm):
    j = pl.program_id(1)

    @pl.when(j == 0)
    def _():
        xn_ref[...] = _rms(x_ref[...], g_ref[...]).astype(BF16)
        acc_ref[...] = jnp.zeros(acc_ref.shape, F32)

    xn = xn_ref[...]
    gate = _dot(xn, wg_ref[...])
    up = _dot(xn, wu_ref[...])
    h = (gate * (1.0 / (1.0 + jnp.exp(-gate))) * up).astype(BF16)
    acc_ref[...] += _dot(h, wo_ref[...])

    @pl.when(j == pl.num_programs(1) - 1)
    def _():
        y = x_ref[...] + 0.5 * acc_ref[...]
        if final_norm:
            y = _rms(y, fg_ref[...])
        o_ref[...] = y


def _ffn(x, g, wi, wo, fg, *, final_norm, tm, tf):
    t, d = x.shape
    dff = wo.shape[0]
    nj = dff // tf
    return pl.pallas_call(
        functools.partial(_ffn_kernel, final_norm=final_norm),
        grid=(t // tm, nj),
        in_specs=[
            pl.BlockSpec((tm, d), lambda i, j: (i, 0)),
            pl.BlockSpec((1, d), lambda i, j: (0, 0)),
            pl.BlockSpec((d, tf), lambda i, j: (0, j)),
            pl.BlockSpec((d, tf), lambda i, j: (0, j + nj)),
            pl.BlockSpec((tf, d), lambda i, j: (j, 0)),
            pl.BlockSpec((1, d), lambda i, j: (0, 0)),
        ],
        out_specs=pl.BlockSpec((tm, d), lambda i, j: (i, 0)),
        out_shape=jax.ShapeDtypeStruct((t, d), F32),
        scratch_shapes=[pltpu.VMEM((tm, d), BF16), pltpu.VMEM((tm, d), F32)],
        compiler_params=_cparams(("parallel", "arbitrary")),
        name="ffn",
    )(x, g, wi, wi, wo, fg)


def _mm_kernel(*refs, norm, residual):
    refs = list(refs)
    a_ref = refs.pop(0)
    g_ref = refs.pop(0) if norm else None
    w_ref = refs.pop(0)
    r_ref = refs.pop(0) if residual else None
    o_ref = refs.pop(0)
    a = a_ref[...]
    if norm:
        a = _rms(a, g_ref[...])
    y = _dot(a.astype(BF16), w_ref[...])
    if residual:
        y = y + r_ref[...]
    o_ref[...] = y


def _matmul(a, w, *, g=None, res=None, tm, tn, name):
    t, k = a.shape
    n = w.shape[1]
    args, specs = [a], [pl.BlockSpec((tm, k), lambda i, j: (i, 0))]
    if g is not None:
        args.append(g)
        specs.append(pl.BlockSpec((1, k), lambda i, j: (0, 0)))
    args.append(w)
    specs.append(pl.BlockSpec((k, tn), lambda i, j: (0, j)))
    if res is not None:
        args.append(res)
        specs.append(pl.BlockSpec((tm, tn), lambda i, j: (i, j)))
    return pl.pallas_call(
        functools.partial(_mm_kernel, norm=g is not None, residual=res is not None),
        grid=(t // tm, n // tn),
        in_specs=specs,
        out_specs=pl.BlockSpec((tm, tn), lambda i, j: (i, j)),
        out_shape=jax.ShapeDtypeStruct((t, n), F32),
        compiler_params=_cparams(("parallel", "parallel")),
        name=name,
    )(*args)


def _proj_kernel(x_ref, g_ref, w_ref, qg_ref, wn_ref, wr_ref, wrr_ref, wuk_ref, kvg_ref, cos_ref,
                 sin_ref, wp_ref,
                 qmla_ref, qd_ref, qn_ref, lat_ref, latk_ref, dkv_ref, dkvb_ref, cmp_ref, sel_ref,
                 selb_ref, win_ref, winb_ref, gates_ref, pool_ref):
    tm = x_ref.shape[0]
    qdt = qmla_ref.dtype
    xn = _rms(x_ref[...], g_ref[...]).astype(BF16)

    def seg(a, n):
        return _dot(xn, w_ref[:, a:a + n])

    def lane_tile(v, k):
        return v[:, k * LANE:(k + 1) * LANE]

    cos_t = cos_ref[...]
    sin_t = sin_ref[...]
    head = seg(S_QC, A_QRANK + A_KVRANK)
    qcn = _rms(head[:, 0:A_QRANK], qg_ref[...]).astype(BF16)
    sc_a = LOG2E * (A_NOPE + A_ROPE) ** -0.5
    hg = 4
    for g0 in range(0, A_HEADS, hg):
        cols = slice(g0 * LANE, (g0 + hg) * LANE)
        q_nope = _dot(qcn, wn_ref[:, cols]).astype(BF16)
        r1 = _dot(qcn, wr_ref[:, cols])
        r2 = _dot(qcn, wrr_ref[:, cols])
        for k in range(hg):
            q_lat = _dot(lane_tile(q_nope, k), wuk_ref[g0 + k])
            q_rope = lane_tile(r1, k) * cos_t + lane_tile(r2, k) * sin_t
            qmla_ref[g0 + k, :, 0:LANE] = (q_lat * sc_a).astype(qdt)
            qmla_ref[g0 + k, :, LANE:2 * LANE] = (q_rope * sc_a).astype(qdt)
    tail = seg(S_CMP, N_WALL - S_CMP)
    c = _rms(head[:, A_QRANK:A_QRANK + A_KVRANK], kvg_ref[...])
    kr = lane_tile(tail, 3) * cos_t + lane_tile(tail, 4) * sin_t
    lat_ref[:, 0:A_KVRANK] = c
    lat_ref[:, A_KVRANK:A_LAT] = kr[:, 0:A_ROPE]
    latk_ref[:, 0:LANE] = c.astype(BF16)
    latk_ref[:, LANE:2 * LANE] = kr.astype(BF16)
    sc_b = LOG2E * B_D ** -0.5
    hb = seg(S_QB, 2 * B_HEADS * LANE)
    for t in range(2 * B_HEADS):
        qd_ref[t] = (lane_tile(hb, t) * sc_b).astype(qdt)
    dkv = seg(S_DKV, 4 * B_D)
    dkv_ref[...] = dkv
    dkvb_ref[...] = dkv.astype(BF16)
    sc_c = LOG2E * C_DK ** -0.5
    hn = seg(S_QN, C_HEADS * LANE)
    for h in range(C_HEADS):
        qn_ref[h] = (lane_tile(hn, h) * sc_c).astype(qdt)
    cmp_r = lane_tile(tail, 0)
    cmp_ref[...] = cmp_r
    pool_ref[0] = jnp.sum(cmp_r.reshape(tm // CMP_BLOCK, CMP_BLOCK, LANE) * wp_ref[...][None], axis=1)
    sel_r = lane_tile(tail, 1)
    sel_ref[...] = sel_r
    selb_ref[...] = sel_r.astype(BF16)
    win_r = lane_tile(tail, 2)
    win_ref[...] = win_r
    winb_ref[...] = win_r.astype(BF16)
    gates_ref[...] = 1.0 / (1.0 + jnp.exp(-lane_tile(tail, 5)))


def _project(x, lw, cos_t, sin_t, *, tm, table_blocks, qdt):
    t, d = x.shape
    nt = t // tm
    const = lambda i: (0, 0)
    row = lambda i: (i, 0)
    one = pl.Buffered(1)
    in_specs = [
        pl.BlockSpec((tm, d), row),
        pl.BlockSpec((1, d), const),
        pl.BlockSpec((d, N_WALL), const, pipeline_mode=one),
        pl.BlockSpec((1, A_QRANK), const),
        pl.BlockSpec((A_QRANK, A_HEADS * LANE), const, pipeline_mode=one),
        pl.BlockSpec((A_QRANK, A_HEADS * LANE), const, pipeline_mode=one),
        pl.BlockSpec((A_QRANK, A_HEADS * LANE), const, pipeline_mode=one),
        pl.BlockSpec((A_HEADS, LANE, LANE), lambda i: (0, 0, 0), pipeline_mode=one),
        pl.BlockSpec((1, A_KVRANK), const),
        pl.BlockSpec((tm, LANE), lambda i: (i % table_blocks, 0)),
        pl.BlockSpec((tm, LANE), lambda i: (i % table_blocks, 0)),
        pl.BlockSpec((CMP_BLOCK, LANE), const),
    ]
    hrow = lambda i: (0, i, 0)
    outs = [
        (jax.ShapeDtypeStruct((A_HEADS, t, 2 * LANE), qdt), pl.BlockSpec((A_HEADS, tm, 2 * LANE), hrow)),
        (jax.ShapeDtypeStruct((2 * B_HEADS, t, LANE), qdt), pl.BlockSpec((2 * B_HEADS, tm, LANE), hrow)),
        (jax.ShapeDtypeStruct((C_HEADS, t, LANE), qdt), pl.BlockSpec((C_HEADS, tm, LANE), hrow)),
        (jax.ShapeDtypeStruct((t, A_LAT), F32), pl.BlockSpec((tm, A_LAT), row)),
        (jax.ShapeDtypeStruct((t, 2 * LANE), BF16), pl.BlockSpec((tm, 2 * LANE), row)),
        (jax.ShapeDtypeStruct((t, 4 * B_D), F32), pl.BlockSpec((tm, 4 * B_D), row)),
        (jax.ShapeDtypeStruct((t, 4 * B_D), BF16), pl.BlockSpec((tm, 4 * B_D), row)),
        (jax.ShapeDtypeStruct((t, LANE), F32), pl.BlockSpec((tm, LANE), row)),
        (jax.ShapeDtypeStruct((t, LANE), F32), pl.BlockSpec((tm, LANE), row)),
        (jax.ShapeDtypeStruct((t, LANE), BF16), pl.BlockSpec((tm, LANE), row)),
        (jax.ShapeDtypeStruct((t, LANE), F32), pl.BlockSpec((tm, LANE), row)),
        (jax.ShapeDtypeStruct((t, LANE), BF16), pl.BlockSpec((tm, LANE), row)),
        (jax.ShapeDtypeStruct((t, LANE), F32), pl.BlockSpec((tm, LANE), row)),
        (jax.ShapeDtypeStruct((nt, tm // CMP_BLOCK, LANE), F32),
         pl.BlockSpec((1, tm // CMP_BLOCK, LANE), lambda i: (i, 0, 0))),
    ]
    res = pl.pallas_call(
        _proj_kernel,
        grid=(nt,),
        in_specs=in_specs,
        out_specs=[o[1] for o in outs],
        out_shape=[o[0] for o in outs],
        compiler_params=_cparams(("parallel",)),
        name="project",
    )(x, lw["mix_norm"], lw["w_all"], lw["q_norm"], lw["wn"], lw["wr"], lw["wrr"], lw["wuk"],
      lw["kv_norm"], cos_t, sin_t, lw["w_pool"])
    keys = ("qmla", "qd", "qn", "lat", "latk", "dkv", "dkvb", "cmp", "sel", "selb", "win", "winb",
            "gates", "pool")
    out = dict(zip(keys, res))
    out["pool"] = out["pool"].reshape(t // CMP_BLOCK, LANE)
    return out


def _prompt_attn_kernel(qmla_ref, qd_ref, qn_ref, latk_ref, dkvb_ref, selb_ref, winb_ref, pool_ref,
                        gates_ref, wuv_ref, dng_ref, lam_ref, exp_ref, fb_ref, fc_ref, posf_ref, o_ref,
                        ma_ref, la_ref, acca_ref, mb_ref, lb_ref, accb_ref, mc_ref, lc_ref, accc_ref,
                        *, kt, lam_init):
    i = pl.program_id(1)
    qb = qmla_ref.shape[1]
    seq = latk_ref.shape[0]
    nblk = pool_ref.shape[0]
    ra, rb, rc = A_HEADS * qb, 2 * B_HEADS * qb, C_HEADS * qb
    q0 = i * qb
    qa = qmla_ref[...].reshape(ra, 2 * LANE)
    qn = qn_ref[...].reshape(rc, LANE)
    qd_aug = jnp.concatenate([qd_ref[...].reshape(rb, LANE), fb_ref[...]], axis=1)
    qn_aug = jnp.concatenate([qn, fc_ref[...]], axis=1)

    def qpos_rows(rows, width):
        return q0 + _mod(lax.broadcasted_iota(jnp.int32, (rows, width), 0), qb)

    slope_c = _alibi_col(rc, qb, C_HEADS)

    kcv = pool_ref[...].astype(BF16)
    blk = lax.broadcasted_iota(jnp.int32, (rc, nblk), 1)
    qp = qpos_rows(rc, nblk)
    cvalid = blk < _div(qp, CMP_BLOCK)
    cdist = (qp - ((blk + 1) * CMP_BLOCK - 1)).astype(F32)
    sc = jnp.where(cvalid, _dot_t(qn, kcv) - slope_c * cdist, MASKED)
    mcmp = jnp.max(sc, axis=1, keepdims=True)
    pc = jnp.where(cvalid, jnp.exp2(sc - mcmp), 0.0)
    pc = pc / jnp.maximum(jnp.sum(pc, axis=1, keepdims=True), 1e-30)
    o_cmp = _dot(pc.astype(BF16), kcv)
    imp = jnp.sum(pc.reshape(C_HEADS, qb, nblk), axis=0)

    wlen = WINDOW + qb
    w0 = pl.multiple_of(jnp.maximum(q0 - WINDOW, 0), qb)
    wrows = winb_ref[pl.ds(w0, wlen), :]
    wd = qpos_rows(rc, wlen) - (w0 + lax.broadcasted_iota(jnp.int32, (rc, wlen), 1))
    sw = jnp.where((wd >= 0) & (wd <= WINDOW), _dot_t(qn, wrows) - slope_c * wd.astype(F32), MASKED)
    pw = jnp.exp2(sw - jnp.max(sw, axis=1, keepdims=True))
    o_win = _dot(pw.astype(BF16), wrows) / jnp.sum(pw, axis=1, keepdims=True)

    _softmax_init(ma_ref, la_ref, acca_ref)
    _softmax_init(mb_ref, lb_ref, accb_ref)
    _softmax_init(mc_ref, lc_ref, accc_ref)

    def tile_ab(j, diagonal):
        k0 = pl.multiple_of(j * kt, kt)
        lat = latk_ref[pl.ds(k0, kt), :]
        dk = dkvb_ref[pl.ds(k0, kt), :]
        sa = _dot_t(qa, lat)
        sb = _dot_t(qd_aug, jnp.concatenate([dk[:, 0:LANE], posf_ref[pl.ds(k0, kt), :]], axis=1))
        if diagonal:
            sa = jnp.where(k0 + lax.broadcasted_iota(jnp.int32, (ra, kt), 1) <= qpos_rows(ra, kt), sa, MASKED)
            sb = jnp.where(k0 + lax.broadcasted_iota(jnp.int32, (rb, kt), 1) <= qpos_rows(rb, kt), sb, MASKED)
        _softmax_step(sa, lambda p: _dot(p, lat[:, 0:LANE]), ma_ref, la_ref, acca_ref)
        _softmax_step(sb, lambda p: _dot(p, dk[:, LANE:2 * LANE]), mb_ref, lb_ref, accb_ref)

    def tile_c(j, diagonal, sel):
        k0 = pl.multiple_of(j * kt, kt)
        ks = selb_ref[pl.ds(k0, kt), :]
        ss = _dot_t(qn_aug, jnp.concatenate([ks, posf_ref[pl.ds(k0, kt), :]], axis=1))
        allowed = _dot(sel, exp_ref[j]) > 0.5
        if diagonal:
            qp1 = q0 + lax.broadcasted_iota(jnp.int32, (qb, kt), 0)
            kp1 = k0 + lax.broadcasted_iota(jnp.int32, (qb, kt), 1)
            allowed = allowed | ((_div(kp1, CMP_BLOCK) == _div(qp1, CMP_BLOCK)) & (kp1 <= qp1))
        ss = jnp.where(allowed[None], ss.reshape(C_HEADS, qb, kt), MASKED).reshape(rc, kt)
        _softmax_step(ss, lambda p: _dot(p, ks), mc_ref, lc_ref, accc_ref)

    n_full = q0 // kt
    tile_ab(n_full, True)
    sel = _topk_mask(imp, cvalid[:qb], N_SEL).astype(BF16)
    tile_c(n_full, True, sel)

    def full_tile(j, carry):
        tile_ab(j, False)
        tile_c(j, False, sel)
        return carry

    lax.fori_loop(0, n_full, full_tile, 0)

    o_lat = _softmax_out(la_ref, acca_ref).astype(BF16).reshape(A_HEADS, qb, LANE)
    for p in range(A_HEADS // 2):
        pair = jnp.concatenate([o_lat[2 * p], o_lat[2 * p + 1]], axis=-1)
        o_ref[:, p * LANE:(p + 1) * LANE] = _dot(pair, wuv_ref[p]).astype(o_ref.dtype)
    o_dif = _softmax_out(lb_ref, accb_ref).reshape(2 * B_HEADS, qb, LANE)
    lam = lam_ref[...]
    for h in range(B_HEADS):
        od = o_dif[2 * h] - lam * o_dif[2 * h + 1]
        od = _rms(od, dng_ref[...]) * (1.0 - lam_init)
        o_ref[:, MIX_A + h * LANE:MIX_A + (h + 1) * LANE] = od.astype(o_ref.dtype)
    o_sel = _softmax_out(lc_ref, accc_ref).reshape(C_HEADS, qb, LANE)
    o_cmp = o_cmp.reshape(C_HEADS, qb, LANE)
    o_win = o_win.reshape(C_HEADS, qb, LANE)
    gates = gates_ref[...]
    for h in range(C_HEADS):
        on = (gates[:, 3 * h:3 * h + 1] * o_cmp[h] + gates[:, 3 * h + 1:3 * h + 2] * o_sel[h]
              + gates[:, 3 * h + 2:3 * h + 3] * o_win[h])
        o_ref[:, MIX_A + MIX_B + h * LANE:MIX_A + MIX_B + (h + 1) * LANE] = on.astype(o_ref.dtype)


def _prompt_attn(pr, lw, lam_row, expand, feats, *, batch, seq, lam_init, kt, qb=128):
    feat_b, feat_c, pos_f = feats
    nq = seq // qb
    nblk = seq // CMP_BLOCK
    hq = lambda b, i: (0, b * nq + i, 0)
    perb = lambda b, i: (b, 0)
    const2 = lambda b, i: (0, 0)
    const3 = lambda b, i: (0, 0, 0)
    ra, rb, rc = A_HEADS * qb, 2 * B_HEADS * qb, C_HEADS * qb
    stat = lambda r: [pltpu.VMEM((r, LANE), F32)] * 3
    return pl.pallas_call(
        functools.partial(_prompt_attn_kernel, kt=kt, lam_init=lam_init),
        grid=(batch, nq),
        in_specs=[
            pl.BlockSpec((A_HEADS, qb, 2 * LANE), hq),
            pl.BlockSpec((2 * B_HEADS, qb, LANE), hq),
            pl.BlockSpec((C_HEADS, qb, LANE), hq),
            pl.BlockSpec((seq, 2 * LANE), perb),
            pl.BlockSpec((seq, 4 * B_D), perb),
            pl.BlockSpec((seq, LANE), perb),
            pl.BlockSpec((seq, LANE), perb),
            pl.BlockSpec((nblk, LANE), perb),
            pl.BlockSpec((qb, LANE), lambda b, i: (b * nq + i, 0)),
            pl.BlockSpec((A_HEADS // 2, 2 * LANE, LANE), const3),
            pl.BlockSpec((1, LANE), const2),
            pl.BlockSpec((1, LANE), const2),
            pl.BlockSpec((seq // kt, nblk, kt), const3),
            pl.BlockSpec((rb, LANE), const2),
            pl.BlockSpec((rc, LANE), const2),
            pl.BlockSpec((seq, LANE), const2),
        ],
        out_specs=pl.BlockSpec((qb, MIX_W), lambda b, i: (b * nq + i, 0)),
        out_shape=jax.ShapeDtypeStruct((batch * seq, MIX_W), BF16),
        scratch_shapes=stat(ra) + stat(rb) + stat(rc),
        compiler_params=_cparams(("parallel", "arbitrary")),
        name="prompt_attn",
    )(pr["qmla"], pr["qd"], pr["qn"], pr["latk"], pr["dkvb"], pr["selb"], pr["winb"], pr["pool"],
      pr["gates"], lw["wuv_pair"], lw["diff_norm"], lam_row, expand, feat_b, feat_c, pos_f)


def _stream_pages(cache_ref, layer, pt_ref, ptn_ref, sem_ref, n_chunks, ppc, dst, consume):
    assert n_chunks % 2 == 0
    b = pl.program_id(0)
    nb = pl.num_programs(0)

    def page_copy(page, slot, p):
        return pltpu.make_async_copy(cache_ref.at[layer, page], dst(slot, p), sem_ref.at[slot])

    def start(table_ref, c, slot):
        for p in range(ppc):
            page_copy(table_ref[0, 0, c * ppc + p], slot, p).start()

    def wait(slot):
        for p in range(ppc):
            page_copy(0, slot, p).wait()

    @pl.when(b == 0)
    def _():
        start(pt_ref, 0, 0)

    def body(c, carry):
        slot = c % 2

        @pl.when(c + 1 < n_chunks)
        def _():
            start(pt_ref, c + 1, 1 - slot)

        @pl.when((c + 1 == n_chunks) & (b + 1 < nb))
        def _():
            start(ptn_ref, 0, 0)

        wait(slot)
        consume(c, slot)
        return carry

    lax.fori_loop(0, n_chunks, body, 0)


def _start_chunks(cache_ref, layer, table_ref, buf_ref, sem_ref, n_chunks, ppc):
    for c in range(n_chunks):
        for p in range(ppc):
            pltpu.make_async_copy(cache_ref.at[layer, table_ref[0, 0, c * ppc + p]], buf_ref.at[c, p],
                                  sem_ref.at[c]).start()


def _wait_chunk(cache_ref, layer, buf_ref, sem_ref, c, ppc):
    for p in range(ppc):
        pltpu.make_async_copy(cache_ref.at[layer, 0], buf_ref.at[c, p], sem_ref.at[c]).wait()


def _new_rows_tile(pad_ref, new_ref):
    pad_ref[...] = jnp.zeros(pad_ref.shape, F32)
    pad_ref[0:new_ref.shape[0], :] = new_ref[...]
    return pad_ref[...].astype(BF16)


def _dec_mla_kernel(pt_ref, ptn_ref, q_ref, new_ref, wuv_ref, cache_ref, o_ref,
                    buf_ref, sem_ref, pad_ref, m_ref, l_ref, acc_ref, *, layer, ppc):
    t = q_ref.shape[1]
    rows = A_HEADS * t
    n_pages = pt_ref.shape[2]
    q = q_ref[...].reshape(rows, 2 * LANE)[:, 0:A_LAT].astype(BF16)
    _softmax_init(m_ref, l_ref, acc_ref)

    def dst(slot, p):
        return buf_ref.at[slot, :, pl.ds(p * PAGE_SIZE, PAGE_SIZE)]

    def consume(c, slot):
        k_t = buf_ref[slot].astype(BF16)
        _softmax_step(_dot(q, k_t), lambda p: _dot_t(p, k_t[0:A_KVRANK, :]), m_ref, l_ref, acc_ref)

    _stream_pages(cache_ref, layer, pt_ref, ptn_ref, sem_ref, n_pages // ppc, ppc, dst, consume)
    kn = _new_rows_tile(pad_ref, new_ref)
    qj = _mod(lax.broadcasted_iota(jnp.int32, (rows, PAGE_SIZE), 0), t)
    kj = lax.broadcasted_iota(jnp.int32, (rows, PAGE_SIZE), 1)
    sn = jnp.where(kj <= qj, _dot_t(q, kn), MASKED)
    _softmax_step(sn, lambda p: _dot(p, kn[:, 0:A_KVRANK]), m_ref, l_ref, acc_ref)
    o_lat = _softmax_out(l_ref, acc_ref).astype(BF16)
    full = _dot(o_lat, wuv_ref[...]).reshape(A_HEADS, t, A_HEADS * A_VDIM)
    own = (_div(lax.broadcasted_iota(jnp.int32, (A_HEADS, 1, A_HEADS * A_VDIM), 2), A_VDIM)
           == lax.broadcasted_iota(jnp.int32, (A_HEADS, 1, A_HEADS * A_VDIM), 0))
    o_ref[...] = jnp.sum(jnp.where(own, full, 0.0), axis=0)


def _dec_diff_kernel(pt_ref, ptn_ref, q_ref, new_ref, dng_ref, lam_ref, cache_ref, o_ref,
                     buf_ref, sem_ref, pad_ref, m_ref, l_ref, acc_ref, *, layer, ppc, past, lam_init):
    t = q_ref.shape[1]
    rows = 2 * B_HEADS * t
    n_pages = pt_ref.shape[2]
    q = q_ref[...].reshape(rows, LANE).astype(BF16)
    slope = _alibi_col(rows, 2 * t, B_HEADS)
    _softmax_init(m_ref, l_ref, acc_ref)
    ck = ppc * PAGE_SIZE
    qpos = past + _mod(lax.broadcasted_iota(jnp.int32, (rows, ck), 0), t)

    def consume(c, slot):
        k = buf_ref[slot].reshape(ck, 4 * B_D).astype(BF16)
        dist = (qpos - (c * ck + lax.broadcasted_iota(jnp.int32, (rows, ck), 1))).astype(F32)
        _softmax_step(_dot_t(q, k[:, 0:LANE]) - slope * dist, lambda p: _dot(p, k[:, LANE:2 * LANE]),
                      m_ref, l_ref, acc_ref)

    _stream_pages(cache_ref, layer, pt_ref, ptn_ref, sem_ref, n_pages // ppc, ppc,
                  lambda slot, p: buf_ref.at[slot, p], consume)
    kn = _new_rows_tile(pad_ref, new_ref)
    dj = (_mod(lax.broadcasted_iota(jnp.int32, (rows, PAGE_SIZE), 0), t)
          - lax.broadcasted_iota(jnp.int32, (rows, PAGE_SIZE), 1))
    sn = jnp.where(dj >= 0, _dot_t(q, kn[:, 0:LANE]) - slope * dj.astype(F32), MASKED)
    _softmax_step(sn, lambda p: _dot(p, kn[:, LANE:2 * LANE]), m_ref, l_ref, acc_ref)
    o = _softmax_out(l_ref, acc_ref)
    lam = lam_ref[...]
    for h in range(B_HEADS):
        od = o[2 * h * t:(2 * h + 1) * t] - lam * o[(2 * h + 1) * t:(2 * h + 2) * t]
        o_ref[:, h * LANE:(h + 1) * LANE] = _rms(od, dng_ref[...]) * (1.0 - lam_init)


def _dec_nsa_kernel(pt_ref, ptn_ref, q_ref, seln_ref, winn_ref, wst_ref, gates_ref, wp_ref, exp_ref,
                    fq_ref, pfs_ref, cmp_cache_ref, sel_cache_ref, o_ref, nwin_ref,
                    bufc_ref, semc_ref, bufs_ref, sems_ref, pad_ref, pool_ref, wbuf_ref,
                    m_ref, l_ref, acc_ref, *, layer, ppc, past):
    t = q_ref.shape[1]
    rows = C_HEADS * t
    n_pages = pt_ref.shape[2]
    n_chunks = n_pages // ppc
    ck = ppc * PAGE_SIZE
    bpc = ck // CMP_BLOCK
    nblk = n_chunks * bpc
    q = q_ref[...].reshape(rows, LANE).astype(BF16)
    slope = _alibi_col(rows, t, C_HEADS)

    b = pl.program_id(0)

    @pl.when(b == 0)
    def _():
        _start_chunks(cmp_cache_ref, layer, pt_ref, bufc_ref, semc_ref, n_chunks, ppc)

    _start_chunks(sel_cache_ref, layer, pt_ref, bufs_ref, sems_ref, n_chunks, ppc)
    for c in range(n_chunks):
        _wait_chunk(cmp_cache_ref, layer, bufc_ref, semc_ref, c, ppc)
        r = bufc_ref[c].reshape(bpc, CMP_BLOCK, LANE) * wp_ref[...][None]
        pool_ref[c * bpc:(c + 1) * bpc, :] = jnp.sum(r, axis=1)

    @pl.when(b + 1 < pl.num_programs(0))
    def _():
        _start_chunks(cmp_cache_ref, layer, ptn_ref, bufc_ref, semc_ref, n_chunks, ppc)

    kcv = pool_ref[...].astype(BF16)
    blk = lax.broadcasted_iota(jnp.int32, (rows, nblk), 1)
    qp = past + _mod(lax.broadcasted_iota(jnp.int32, (rows, nblk), 0), t)
    cvalid = blk < _div(qp, CMP_BLOCK)
    cdist = (qp - ((blk + 1) * CMP_BLOCK - 1)).astype(F32)
    sc = jnp.where(cvalid, _dot_t(q, kcv) - slope * cdist, MASKED)
    pc = jnp.where(cvalid, jnp.exp2(sc - jnp.max(sc, axis=1, keepdims=True)), 0.0)
    pc = pc / jnp.maximum(jnp.sum(pc, axis=1, keepdims=True), 1e-30)
    o_cmp = _dot(pc.astype(BF16), kcv)
    imp = jnp.sum(pc.reshape(C_HEADS, t, nblk), axis=0)
    sel = _topk_mask_by_rank(imp, cvalid[:t], N_SEL)
    sel_rows = jnp.tile(sel, (C_HEADS, 1))

    _softmax_init(m_ref, l_ref, acc_ref)
    q_aug = jnp.concatenate([q, fq_ref[...]], axis=1)

    for c in range(n_chunks):
        _wait_chunk(sel_cache_ref, layer, bufs_ref, sems_ref, c, ppc)
        k = bufs_ref[c].reshape(ck, LANE).astype(BF16)
        chosen = _dot(sel_rows[:, c * bpc:(c + 1) * bpc].astype(BF16), exp_ref[...]) > 0.5
        s = jnp.where(chosen, _dot_t(q_aug, jnp.concatenate([k, pfs_ref[c]], axis=1)), MASKED)
        _softmax_step(s, lambda p, k=k: _dot(p, k), m_ref, l_ref, acc_ref)
    kn = _new_rows_tile(pad_ref, seln_ref)
    kj = lax.broadcasted_iota(jnp.int32, (rows, PAGE_SIZE), 1)
    qj = _mod(lax.broadcasted_iota(jnp.int32, (rows, PAGE_SIZE), 0), t)
    sn = jnp.where(kj <= qj, _dot_t(q, kn) + slope * kj.astype(F32), MASKED)
    _softmax_step(sn, lambda p: _dot(p, kn), m_ref, l_ref, acc_ref)
    o_sel = _softmax_out(l_ref, acc_ref)

    wb = wst_ref.shape[0]
    wbuf_ref[...] = jnp.zeros(wbuf_ref.shape, F32)
    wbuf_ref[0:wb, :] = wst_ref[...]
    wbuf_ref[wb:wb + t, :] = winn_ref[...]
    wrows = wbuf_ref[...].astype(BF16)
    wl = wbuf_ref.shape[0]
    idx = lax.broadcasted_iota(jnp.int32, (rows, wl), 1)
    wpos = jnp.where(idx < wb, past - wb + idx, past + idx - wb)
    wd = past + _mod(lax.broadcasted_iota(jnp.int32, (rows, wl), 0), t) - wpos
    ok = (wd >= 0) & (wd <= WINDOW) & (idx < wb + t) & (wpos >= 0)
    sw = jnp.where(ok, _dot_t(q, wrows) - slope * wd.astype(F32), MASKED)
    pw = jnp.exp2(sw - jnp.max(sw, axis=1, keepdims=True))
    o_win = _dot(pw.astype(BF16), wrows) / jnp.sum(pw, axis=1, keepdims=True)
    nwin_ref[0:wb - t, :] = wst_ref[t:wb, :]
    nwin_ref[wb - t:wb, :] = winn_ref[...]

    gates = gates_ref[...]
    for h in range(C_HEADS):
        r = slice(h * t, (h + 1) * t)
        o_ref[:, h * LANE:(h + 1) * LANE] = (gates[:, 3 * h:3 * h + 1] * o_cmp[r]
                                              + gates[:, 3 * h + 1:3 * h + 2] * o_sel[r]
                                              + gates[:, 3 * h + 2:3 * h + 3] * o_win[r])


def _sample_attn(sp, lw, lam_row, expand_s, feats, caches, win_state, page_table3, *, layer, lam_init,
                 past, ppc):
    cache_mla, cache_diff, cache_cmp, cache_sel = caches
    feat_s, pos_s = feats
    db, _, n_pages = page_table3.shape
    ts = sp["lat"].shape[0]
    t = ts // db
    wb = win_state.shape[2]
    pt_spec = pl.BlockSpec((1, 1, n_pages), lambda b: (b, 0, 0), memory_space=pltpu.SMEM)
    ptn_spec = pl.BlockSpec((1, 1, n_pages), lambda b: (jnp.minimum(b + 1, db - 1), 0, 0),
                            memory_space=pltpu.SMEM)
    hq = lambda b: (0, b, 0)
    row = lambda b: (b, 0)
    const2 = lambda b: (0, 0)
    any_spec = pl.BlockSpec(memory_space=pl.ANY)
    stat = lambda r: [pltpu.VMEM((r, LANE), F32)] * 3
    ck = ppc * PAGE_SIZE
    dma2 = pltpu.SemaphoreType.DMA((2,))

    def stream_scratch(width):
        return [pltpu.VMEM((2, ppc, PAGE_SIZE, width), F32), dma2]

    o_a = pl.pallas_call(
        functools.partial(_dec_mla_kernel, layer=layer, ppc=ppc),
        grid=(db,),
        in_specs=[pt_spec, ptn_spec,
                  pl.BlockSpec((A_HEADS, t, 2 * LANE), hq),
                  pl.BlockSpec((t, A_LAT), row),
                  pl.BlockSpec((A_KVRANK, A_HEADS * A_VDIM), const2),
                  any_spec],
        out_specs=pl.BlockSpec((t, MIX_A), row),
        out_shape=jax.ShapeDtypeStruct((ts, MIX_A), F32),
        scratch_shapes=([pltpu.VMEM((2, A_LAT, ck), F32), dma2, pltpu.VMEM((PAGE_SIZE, A_LAT), F32)]
                        + stat(A_HEADS * t)),
        compiler_params=_cparams(("arbitrary",)),
        name="sample_mla",
    )(page_table3, page_table3, sp["qmla"], sp["lat"], lw["wuv_flat"], cache_mla)

    o_b = pl.pallas_call(
        functools.partial(_dec_diff_kernel, layer=layer, ppc=ppc, past=past, lam_init=lam_init),
        grid=(db,),
        in_specs=[pt_spec, ptn_spec,
                  pl.BlockSpec((2 * B_HEADS, t, LANE), hq),
                  pl.BlockSpec((t, 4 * B_D), row),
                  pl.BlockSpec((1, LANE), const2),
                  pl.BlockSpec((1, LANE), const2),
                  any_spec],
        out_specs=pl.BlockSpec((t, MIX_B), row),
        out_shape=jax.ShapeDtypeStruct((ts, MIX_B), F32),
        scratch_shapes=(stream_scratch(4 * B_D) + [pltpu.VMEM((PAGE_SIZE, 4 * B_D), F32)]
                        + stat(2 * B_HEADS * t)),
        compiler_params=_cparams(("arbitrary",)),
        name="sample_diff",
    )(page_table3, page_table3, sp["qd"], sp["dkv"], lw["diff_norm"], lam_row, cache_diff)

    n_chunks = n_pages // ppc
    bpc = ppc * PAGE_SIZE // CMP_BLOCK
    rows_c = C_HEADS * t
    o_c, new_win = pl.pallas_call(
        functools.partial(_dec_nsa_kernel, layer=layer, ppc=ppc, past=past),
        grid=(db,),
        in_specs=[pt_spec, ptn_spec,
                  pl.BlockSpec((C_HEADS, t, LANE), hq),
                  pl.BlockSpec((t, LANE), row),
                  pl.BlockSpec((t, LANE), row),
                  pl.BlockSpec((None, None, wb, LANE), lambda b: (layer, b, 0, 0)),
                  pl.BlockSpec((t, LANE), row),
                  pl.BlockSpec((CMP_BLOCK, LANE), const2),
                  pl.BlockSpec((bpc, ppc * PAGE_SIZE), const2),
                  pl.BlockSpec((rows_c, LANE), const2),
                  pl.BlockSpec((n_chunks, ck, LANE), lambda b: (0, 0, 0), pipeline_mode=pl.Buffered(1)),
                  any_spec, any_spec],
        out_specs=[pl.BlockSpec((t, MIX_C), row),
                   pl.BlockSpec((None, wb, LANE), lambda b: (b, 0, 0))],
        out_shape=[jax.ShapeDtypeStruct((ts, MIX_C), F32),
                   jax.ShapeDtypeStruct((db, wb, LANE), F32)],
        scratch_shapes=(2 * [pltpu.VMEM((n_chunks, ppc, PAGE_SIZE, LANE), F32),
                             pltpu.SemaphoreType.DMA((n_chunks,))]
                        + [pltpu.VMEM((PAGE_SIZE, LANE), F32),
                           pltpu.VMEM((n_chunks * bpc, LANE), F32),
                           pltpu.VMEM((wb + PAGE_SIZE, LANE), F32)]
                        + stat(rows_c)),
        compiler_params=_cparams(("arbitrary",)),
        name="sample_nsa",
    )(page_table3, page_table3, sp["qn"], sp["sel"], sp["win"], win_state, sp["gates"], lw["w_pool"],
      expand_s, feat_s, pos_s, cache_cmp, cache_sel)
    mixed = jnp.concatenate([o_a, o_b, o_c], axis=-1).astype(BF16)
    return mixed, new_win


def _xattn_kernel(x_ref, g_ref, wq_ref, mem_ref, wo_ref, o_ref, *, small):
    grp, tq, d = x_ref.shape
    x = x_ref[...].reshape(grp * tq, d)
    q = _dot(_rms(x, g_ref[...]).astype(BF16), wq_ref[...]) * (M_DH ** -0.5)
    hd = M_HEADS * M_DH
    outs = []
    for gi in range(grp):
        kv = mem_ref[gi]
        qg = q[gi * tq:(gi + 1) * tq]
        if not small:
            kv = kv.astype(BF16)
            qg = qg.astype(BF16)
        heads = []
        for h in range(M_HEADS):
            s = _dot_t(qg[:, h * M_DH:(h + 1) * M_DH], kv[:, h * M_DH:(h + 1) * M_DH])
            p = jnp.exp(s - jnp.max(s, axis=1, keepdims=True))
            inv = 1.0 / jnp.sum(p, axis=1, keepdims=True)
            if not small:
                p = p.astype(BF16)
            heads.append(_dot(p, kv[:, hd + h * M_DH:hd + (h + 1) * M_DH]) * inv)
        outs.append(jnp.concatenate(heads, axis=-1))
    attn = outs[0] if grp == 1 else jnp.concatenate(outs, axis=0)
    o_ref[...] = (x + _dot(attn.astype(BF16), wo_ref[...])).reshape(grp, tq, d)


def _xattn(x3, g, wq, mem, wo, *, layer, grp, tq):
    nb, t, d = x3.shape
    nm = mem.shape[-2]
    hd2 = mem.shape[-1]
    if mem.ndim == 4:
        mem_spec = pl.BlockSpec((None, grp, nm, hd2), lambda b, i: (layer, b, 0, 0))
    else:
        mem_spec = pl.BlockSpec((grp, nm, hd2), lambda b, i: (b, 0, 0))
    return pl.pallas_call(
        functools.partial(_xattn_kernel, small=tq < 16),
        grid=(nb // grp, t // tq),
        in_specs=[pl.BlockSpec((grp, tq, d), lambda b, i: (b, i, 0)),
                  pl.BlockSpec((1, d), lambda b, i: (0, 0)),
                  pl.BlockSpec((d, M_HEADS * M_DH), lambda b, i: (0, 0)),
                  mem_spec,
                  pl.BlockSpec((M_HEADS * M_DH, d), lambda b, i: (0, 0))],
        out_specs=pl.BlockSpec((grp, tq, d), lambda b, i: (b, i, 0)),
        out_shape=jax.ShapeDtypeStruct((nb, t, d), F32),
        compiler_params=_cparams(("parallel", "parallel")),
        name="xattn",
    )(x3, g, wq, mem, wo)


def _wall_columns():
    offs = np.concatenate([[0], np.cumsum(IN_SIZES)])
    o_qc, o_ckv, o_kr, o_qb, o_dkv, o_qn, o_cmp, o_sel, o_win, o_gl = offs[:-1]
    idx = np.zeros(N_WALL, np.int32)
    sgn = np.zeros(N_WALL, np.float32)

    def put(dst, src, sign=1.0):
        src = np.asarray(src)
        idx[dst:dst + len(src)] = src
        sgn[dst:dst + len(src)] = sign

    put(S_QC, o_qc + np.arange(A_QRANK))
    put(S_CKV, o_ckv + np.arange(A_KVRANK))
    for h in range(B_HEADS):
        put(S_QB + (2 * h) * LANE, o_qb + h * 2 * B_D + np.arange(B_D))
        put(S_QB + (2 * h + 1) * LANE + B_D, o_qb + h * 2 * B_D + B_D + np.arange(B_D))
    put(S_DKV, o_dkv + np.arange(4 * B_D))
    for h in range(C_HEADS):
        put(S_QN + h * LANE, o_qn + h * C_DK + np.arange(C_DK))
    put(S_CMP, o_cmp + np.arange(LANE))
    put(S_SEL, o_sel + np.arange(LANE))
    put(S_WIN, o_win + np.arange(LANE))
    half = A_ROPE // 2
    put(S_KR, o_kr + np.arange(A_ROPE))
    put(S_KRR, o_kr + half + np.arange(half), -1.0)
    put(S_KRR + half, o_kr + np.arange(half))
    put(S_GL, o_gl + np.arange(3 * C_HEADS))
    return idx, sgn


def _uq_columns():
    width = A_HEADS * LANE
    per = A_NOPE + A_ROPE
    half = A_ROPE // 2
    idx = np.zeros((3, width), np.int32)
    sgn = np.zeros((3, width), np.float32)
    for h in range(A_HEADS):
        idx[0, h * LANE:h * LANE + A_NOPE] = h * per + np.arange(A_NOPE)
        sgn[0, h * LANE:h * LANE + A_NOPE] = 1.0
        idx[1, h * LANE:h * LANE + A_ROPE] = h * per + A_NOPE + np.arange(A_ROPE)
        sgn[1, h * LANE:h * LANE + A_ROPE] = 1.0
        idx[2, h * LANE:h * LANE + half] = h * per + A_NOPE + half + np.arange(half)
        sgn[2, h * LANE:h * LANE + half] = -1.0
        idx[2, h * LANE + half:h * LANE + A_ROPE] = h * per + A_NOPE + np.arange(half)
        sgn[2, h * LANE + half:h * LANE + A_ROPE] = 1.0
    return idx, sgn


def _wout_rows():
    idx = np.zeros(MIX_W, np.int32)
    sgn = np.zeros(MIX_W, np.float32)
    idx[:MIX_A + MIX_B] = np.arange(MIX_A + MIX_B)
    sgn[:MIX_A + MIX_B] = 1.0
    for h in range(C_HEADS):
        dst = MIX_A + MIX_B + h * LANE + C_DK
        idx[dst:dst + C_DV] = MIX_A + MIX_B + h * C_DV + np.arange(C_DV)
        sgn[dst:dst + C_DV] = 1.0
    return idx, sgn


def _gather(w, idx, sgn, axis):
    parts, n, a = [], len(idx), 0
    while a < n:
        b = a + 1
        if sgn[a] == 0:
            while b < n and sgn[b] == 0:
                b += 1
            shape = (w.shape[0], b - a) if axis == 1 else (b - a, w.shape[1])
            parts.append(jnp.zeros(shape, w.dtype))
        else:
            while b < n and sgn[b] == sgn[a] and idx[b] == idx[b - 1] + 1:
                b += 1
            piece = lax.slice_in_dim(w, int(idx[a]), int(idx[a]) + (b - a), axis=axis)
            parts.append(piece if sgn[a] > 0 else -piece)
        a = b
    return jnp.concatenate(parts, axis=axis).astype(BF16)


def _gather_cols(w, idx, sgn):
    return _gather(w, idx, sgn, 1)


def _layer_weights(l, p):
    d = p["w_in"].shape[1]
    lw = {}
    lw["mix_norm"] = p["mix_norm"][l].reshape(1, d)
    lw["w_all"] = _gather_cols(p["w_in"][l], *_wall_columns())
    lw["q_norm"] = p["mla_q_norm"][l].reshape(1, A_QRANK)
    uidx, usgn = _uq_columns()
    lw["wn"], lw["wr"], lw["wrr"] = (_gather_cols(p["mla_w_uq"][l], uidx[k], usgn[k]) for k in range(3))
    wuk = jnp.transpose(p["mla_w_uk"][l], (1, 2, 0))
    lw["wuk"] = jnp.pad(wuk, ((0, 0), (0, LANE - A_NOPE), (0, 0))).astype(BF16)
    lw["kv_norm"] = p["mla_kv_norm"][l].reshape(1, A_KVRANK)
    wuv = p["mla_w_uv"][l]
    lw["wuv_flat"] = wuv.reshape(A_KVRANK, A_HEADS * A_VDIM).astype(BF16)
    z = jnp.zeros((A_KVRANK, A_VDIM), F32)
    pairs = [jnp.concatenate([jnp.concatenate([wuv[:, 2 * j], z], axis=1),
                              jnp.concatenate([z, wuv[:, 2 * j + 1]], axis=1)], axis=0)
             for j in range(A_HEADS // 2)]
    lw["wuv_pair"] = jnp.stack(pairs).astype(BF16)
    lw["diff_norm"] = p["diff_norm"][l].reshape(1, 2 * B_D)
    lw["w_pool"] = jnp.broadcast_to(p["nsa_w_cmp"][l][:, None], (CMP_BLOCK, LANE))
    ridx, rsgn = _wout_rows()
    lw["w_out"] = _gather(p["w_out"][l], ridx, rsgn, 0)
    for k in ("ffn1_norm", "ffn2_norm", "xattn_norm", "mem_norm"):
        lw[k] = p[k][l].reshape(1, d)
    for k in ("ffn1_wi", "ffn1_wo", "ffn2_wi", "ffn2_wo", "xattn_wq", "xattn_wkv", "xattn_wo"):
        lw[k] = p[k][l].astype(BF16)
    return lw


def _rope_tables(pos):
    half = A_ROPE // 2
    inv = ROPE_BASE ** (-jnp.arange(half, dtype=F32) / half)
    ang = pos.astype(F32)[:, None] * inv
    pad = jnp.zeros((pos.shape[0], LANE - A_ROPE), F32)
    cos_t = jnp.concatenate([jnp.cos(ang), jnp.cos(ang), pad], axis=1)
    sin_t = jnp.concatenate([jnp.sin(ang), jnp.sin(ang), pad], axis=1)
    return cos_t, sin_t


def _slope_feats(n_heads, tiles_per_head, rows_per_tile):
    c = LOG2E * jnp.exp2(-(8.0 / n_heads) * (jnp.arange(n_heads, dtype=F32) + 1.0))
    c1 = c.astype(BF16).astype(F32)
    c2 = (c - c1).astype(BF16).astype(F32)
    c3 = (c - c1 - c2).astype(BF16).astype(F32)
    row = jnp.stack([c1, c2, c3, c1, c2, c3], axis=1)
    row = jnp.pad(row, ((0, 0), (0, LANE - 6)))
    return jnp.repeat(row, tiles_per_head * rows_per_tile, axis=0).astype(BF16)


def _pos_feats(pos):
    pos = np.asarray(pos, np.int64)
    lo = pos % CMP_BLOCK
    hi = pos - lo
    assert np.all(np.abs(hi) // CMP_BLOCK <= 256)
    f = np.zeros((len(pos), LANE), np.float32)
    f[:, 0:3] = hi[:, None]
    f[:, 3:6] = lo[:, None]
    return jnp.asarray(f, BF16)


def _block_expander(n_blocks, n_keys):
    e = (np.arange(n_keys)[None, :] // CMP_BLOCK) == np.arange(n_blocks)[:, None]
    return e.astype(np.float32)


def _tile(n, pref):
    t = min(n, pref)
    assert n % t == 0, (n, t)
    return t


def kernel(x_prompt, x_sample, mem_prompt, cache_mla, cache_diff, cache_nsa_cmp, cache_nsa_sel, state_nsa_win, cache_mem, page_table, ffn1_norm, ffn1_wi, ffn1_wo, mix_norm, w_in, mla_q_norm, mla_w_uq, mla_kv_norm, mla_w_uk, mla_w_uv, diff_lam, diff_norm, nsa_w_cmp, w_out, xattn_norm, mem_norm, xattn_wq, xattn_wkv, xattn_wo, ffn2_norm, ffn2_wi, ffn2_wo, final_norm):
    params = dict(ffn1_norm=ffn1_norm, ffn1_wi=ffn1_wi, ffn1_wo=ffn1_wo, mix_norm=mix_norm, w_in=w_in,
                  mla_q_norm=mla_q_norm, mla_w_uq=mla_w_uq, mla_kv_norm=mla_kv_norm, mla_w_uk=mla_w_uk,
                  mla_w_uv=mla_w_uv, diff_norm=diff_norm, nsa_w_cmp=nsa_w_cmp, w_out=w_out,
                  xattn_norm=xattn_norm, mem_norm=mem_norm, xattn_wq=xattn_wq, xattn_wkv=xattn_wkv,
                  xattn_wo=xattn_wo, ffn2_norm=ffn2_norm, ffn2_wi=ffn2_wi, ffn2_wo=ffn2_wo)
    batch, seq, d = x_prompt.shape
    db, ds, _ = x_sample.shape
    depth = w_in.shape[0]
    n_mem = mem_prompt.shape[1]
    n_pages = page_table.shape[1]
    past = n_pages * PAGE_SIZE
    tp, ts = batch * seq, db * ds
    assert ds % SUBLANE == 0 and seq % 128 == 0 and state_nsa_win.shape[2] == min(WINDOW, past)
    tm_p = _tile(seq, 512)
    tm_s = _tile(ts, 256)
    assert tm_s % ds == 0
    tf = _tile(ffn1_wo.shape[1], 512)
    tn = _tile(d, 1024)

    cos_p, sin_p = _rope_tables(jnp.arange(seq))
    cos_s, sin_s = _rope_tables(past + (jnp.arange(tm_s) % ds))
    kt = min(512, seq)
    expand_p = jnp.asarray(_block_expander(seq // CMP_BLOCK, seq).reshape(seq // CMP_BLOCK, seq // kt, kt)
                           .transpose(1, 0, 2), BF16)
    feats_p = (_slope_feats(B_HEADS, 2, 128), _slope_feats(C_HEADS, 1, 128), _pos_feats(np.arange(seq)))
    ppc = min(32, n_pages // 2)
    assert n_pages % (2 * ppc) == 0
    cache_mla_t = jnp.swapaxes(cache_mla, 2, 3)
    expand_s = jnp.asarray(_block_expander(ppc * PAGE_SIZE // CMP_BLOCK, ppc * PAGE_SIZE), BF16)
    feats_s = (_slope_feats(C_HEADS, 1, ds),
               _pos_feats(np.arange(past) - past).reshape(n_pages // ppc, ppc * PAGE_SIZE, LANE))
    page_table3 = page_table.reshape(db, 1, n_pages)
    fg = final_norm.reshape(1, d)

    xp = x_prompt.reshape(tp, d)
    xs = x_sample.reshape(ts, d)
    mem2 = mem_prompt.reshape(batch * n_mem, d)
    outs = {k: [] for k in ("mla_p", "mla_s", "diff_p", "diff_s", "cmp_p", "cmp_s", "sel_p", "sel_s",
                            "win_p", "win_s", "mem_p")}
    for l in range(depth):
        lw = _layer_weights(l, params)
        lam_init = 0.8 - 0.6 * math.exp(-0.3 * l)
        dl = diff_lam[l].astype(F32)
        lam = jnp.exp(jnp.sum(dl[0] * dl[1])) - jnp.exp(jnp.sum(dl[2] * dl[3])) + lam_init
        lam_row = jnp.full((1, LANE), lam, F32)
        xp = _ffn(xp, lw["ffn1_norm"], lw["ffn1_wi"], lw["ffn1_wo"], fg, final_norm=False, tm=tm_p, tf=tf)
        xs = _ffn(xs, lw["ffn1_norm"], lw["ffn1_wi"], lw["ffn1_wo"], fg, final_norm=False, tm=tm_s, tf=tf)
        pr = _project(xp, lw, cos_p, sin_p, tm=tm_p, table_blocks=seq // tm_p, qdt=BF16)
        mix_p = _prompt_attn(pr, lw, lam_row, expand_p, feats_p, batch=batch, seq=seq, lam_init=lam_init,
                             kt=kt)
        xp = _matmul(mix_p, lw["w_out"], res=xp, tm=tm_p, tn=tn, name="w_out")
        sp = _project(xs, lw, cos_s, sin_s, tm=tm_s, table_blocks=1, qdt=F32)
        mix_s, new_win = _sample_attn(sp, lw, lam_row, expand_s, feats_s,
                                      (cache_mla_t, cache_diff, cache_nsa_cmp, cache_nsa_sel),
                                      state_nsa_win, page_table3, layer=l, lam_init=lam_init, past=past,
                                      ppc=ppc)
        xs = _matmul(mix_s, lw["w_out"], res=xs, tm=tm_s, tn=tn, name="w_out_s")
        mem_kv = _matmul(mem2, lw["xattn_wkv"], g=lw["mem_norm"], tm=_tile(batch * n_mem, 512),
                         tn=_tile(lw["xattn_wkv"].shape[1], 1024), name="mem_kv")
        mem_kv = mem_kv.reshape(batch, n_mem, -1)
        xp = _xattn(xp.reshape(batch, seq, d), lw["xattn_norm"], lw["xattn_wq"], mem_kv, lw["xattn_wo"],
                    layer=l, grp=1, tq=tm_p).reshape(tp, d)
        xs = _xattn(xs.reshape(db, ds, d), lw["xattn_norm"], lw["xattn_wq"], cache_mem, lw["xattn_wo"],
                    layer=l, grp=_tile(db, 8), tq=ds).reshape(ts, d)
        last = l == depth - 1
        xp = _ffn(xp, lw["ffn2_norm"], lw["ffn2_wi"], lw["ffn2_wo"], fg, final_norm=last, tm=tm_p, tf=tf)
        xs = _ffn(xs, lw["ffn2_norm"], lw["ffn2_wi"], lw["ffn2_wo"], fg, final_norm=last, tm=tm_s, tf=tf)
        wlen = min(WINDOW, seq)
        outs["mla_p"].append(pr["lat"].reshape(batch, seq, A_LAT))
        outs["mla_s"].append(sp["lat"].reshape(db, ds, A_LAT))
        outs["diff_p"].append(pr["dkv"].reshape(batch, seq, 4 * B_D))
        outs["diff_s"].append(sp["dkv"].reshape(db, ds, 4 * B_D))
        outs["cmp_p"].append(pr["cmp"].reshape(batch, seq, LANE))
        outs["cmp_s"].append(sp["cmp"].reshape(db, ds, LANE))
        outs["sel_p"].append(pr["sel"].reshape(batch, seq, LANE))
        outs["sel_s"].append(sp["sel"].reshape(db, ds, LANE))
        outs["win_p"].append(pr["win"].reshape(batch, seq, LANE)[:, seq - wlen:])
        outs["win_s"].append(new_win)
        outs["mem_p"].append(mem_kv)
    st = {k: jnp.stack(v) for k, v in outs.items()}
    return (xp.reshape(batch, seq, d), xs.reshape(db, ds, d),
            st["mla_p"], st["mla_s"], st["diff_p"], st["diff_s"], st["cmp_p"], st["cmp_s"],
            st["sel_p"], st["sel_s"], st["win_p"], st["win_s"], st["mem_p"])
```
